```python
import math
import jax, jax.numpy as jnp
from jax import lax
import numpy as np

D_MODEL = 1024
BATCH = 16
SEQ = 2048
DEPTH = 4

GRID_W = 64
CTX_LEN = 256
D_MIX = D_MODEL
S5_WIDTH = D_MIX // 4
S5_GROUP = 16
S5_GROUPS = S5_WIDTH // S5_GROUP
S5_STATE = 64
LRU_WIDTH = D_MIX // 4
LRU_HEADS = 8
LRU_HEAD_W = LRU_WIDTH // LRU_HEADS
LRU_CONV = 4
LRU_C = 8.0
MLA_HEADS = 8
MLA_NOPE = 64
MLA_ROPE = 32
MLA_QK = MLA_NOPE + MLA_ROPE
MLA_V = 64
MLA_WIDTH = MLA_HEADS * MLA_V
MLA_Q_RANK = 256
MLA_KV_RANK = 128
ATTN_SCALE = MLA_QK ** -0.5
ROPE_AXIS = MLA_ROPE // 2
ROPE_BASE = 10000.0
Q_BLOCK = 128
EPS = 1e-6
IN_SPLITS = (S5_WIDTH, S5_WIDTH, LRU_WIDTH, LRU_WIDTH, MLA_Q_RANK, MLA_KV_RANK, MLA_ROPE, MLA_WIDTH)
D_IN = sum(IN_SPLITS)
SPLIT_IDX = tuple(int(i) for i in np.cumsum(IN_SPLITS)[:-1])

kernel_name = 'hybrid_s5_rglru_mla_prefix_dit'


def _rmsnorm(x, g):
    xf = x.astype(jnp.float32)
    y = xf * lax.rsqrt(jnp.mean(xf * xf, axis=-1, keepdims=True) + EPS)
    return (y * g.astype(jnp.float32)).astype(x.dtype)


def _flip(t, rev):
    return jnp.flip(t, axis=1) if rev else t


def _linear_combine(prev, cur):
    a1, b1 = prev
    a2, b2 = cur
    return a1 * a2, a2 * b1 + b2


def _complex_combine(prev, cur):
    a1r, a1i, b1r, b1i = prev
    a2r, a2i, b2r, b2i = cur
    return (a1r * a2r - a1i * a2i, a1r * a2i + a1i * a2r,
            a2r * b1r - a2i * b1i + b2r, a2r * b1i + a2i * b1r + b2i)


def _s5_direction(u, h0, lam_re, lam_im, log_dt, b_re, b_im, c_re, c_im, with_out):
    f32 = jnp.float32
    lam_re, lam_im = lam_re.astype(f32), lam_im.astype(f32)
    b_re, b_im = b_re.astype(f32), b_im.astype(f32)
    c_re, c_im = c_re.astype(f32), c_im.astype(f32)
    dt = jnp.exp(log_dt.astype(f32))[:, None]
    mag = jnp.exp(lam_re * dt)
    ab_re, ab_im = mag * jnp.cos(lam_im * dt), mag * jnp.sin(lam_im * dt)
    den = lam_re * lam_re + lam_im * lam_im
    nr = ab_re - 1.0
    coef_re = (nr * lam_re + ab_im * lam_im) / den
    coef_im = (ab_im * lam_re - nr * lam_im) / den
    bb_re = coef_re[..., None] * b_re - coef_im[..., None] * b_im
    bb_im = coef_re[..., None] * b_im + coef_im[..., None] * b_re
    x_re = jnp.einsum('gpc,blgc->blgp', bb_re, u)
    x_im = jnp.einsum('gpc,blgc->blgp', bb_im, u)
    if h0 is not None:
        h0_re, h0_im = h0
        x_re = x_re.at[:, 0].add(ab_re * h0_re - ab_im * h0_im)
        x_im = x_im.at[:, 0].add(ab_re * h0_im + ab_im * h0_re)
    seq_len = u.shape[1]
    a_re = jnp.broadcast_to(ab_re, (1, seq_len) + ab_re.shape)
    a_im = jnp.broadcast_to(ab_im, (1, seq_len) + ab_im.shape)
    _, _, h_re, h_im = lax.associative_scan(_complex_combine, (a_re, a_im, x_re, x_im), axis=1)
    last = (h_re[:, -1], h_im[:, -1])
    if not with_out:
        return None, last
    y = jnp.einsum('gcp,blgp->blgc', c_re, h_re) - jnp.einsum('gcp,blgp->blgc', c_im, h_im)
    return y, last


def _s5_mixer(u_lat, u_ctx, lam_re, lam_im, log_dt, b_re, b_im, c_re, c_im, d_skip, glu_w, glu_b, with_ctx):
    f32 = jnp.float32
    ul = u_lat.astype(f32)
    uc = u_ctx.astype(f32)
    gl = ul.reshape(ul.shape[0], ul.shape[1], S5_GROUPS, S5_GROUP)
    gc = uc.reshape(uc.shape[0], uc.shape[1], S5_GROUPS, S5_GROUP)
    d = d_skip.astype(f32)
    y_lat = d * ul
    y_ctx = d * uc if with_ctx else None
    for di in range(2):
        rev = di == 1
        prm = (lam_re[di], lam_im[di], log_dt[di], b_re[di], b_im[di], c_re[di], c_im[di])
        yc, hc = _s5_direction(_flip(gc, rev), None, *prm, with_out=with_ctx)
        yl, _ = _s5_direction(_flip(gl, rev), hc, *prm, with_out=True)
        y_lat = y_lat + _flip(yl, rev).reshape(y_lat.shape)
        if with_ctx:
            y_ctx = y_ctx + _flip(yc, rev).reshape(y_ctx.shape)

    def glu(y):
        g = y @ glu_w + glu_b
        return g[..., :S5_WIDTH] * jax.nn.sigmoid(g[..., S5_WIDTH:])

    return glu(y_lat), (glu(y_ctx) if with_ctx else None)


def _conv_centred(x, w, b):
    seq_len = x.shape[1]
    left = LRU_CONV // 2
    xp = jnp.pad(x, ((0, 0), (left, LRU_CONV - 1 - left), (0, 0)))
    y = b
    for k in range(LRU_CONV):
        y = y + xp[:, k:k + seq_len] * w[k]
    return y


def _rglru_direction(x, h0, w_a, b_a, w_x, b_x, lam):
    bsz, seq_len, _ = x.shape
    xh = x.reshape(bsz, seq_len, LRU_HEADS, LRU_HEAD_W)
    r = jax.nn.sigmoid(jnp.einsum('blhi,hij->blhj', xh, w_a).reshape(bsz, seq_len, LRU_WIDTH) + b_a)
    i = jax.nn.sigmoid(jnp.einsum('blhi,hij->blhj', xh, w_x).reshape(bsz, seq_len, LRU_WIDTH) + b_x)
    log_a = -LRU_C * r * jax.nn.softplus(-lam.astype(jnp.float32))
    a = jnp.exp(log_a)
    mult = jnp.sqrt(jnp.maximum(-jnp.expm1(2.0 * log_a), 0.0))
    b = mult * (i * x)
    if h0 is not None:
        b = b.at[:, 0].add(a[:, 0] * h0)
    _, h = lax.associative_scan(_linear_combine, (a, b), axis=1)
    return h, h[:, -1]


def _rglru_mixer(x_lat, x_ctx, conv_w, conv_b, w_a, b_a, w_x, b_x, lam, with_ctx):
    f32 = jnp.float32
    xl = _conv_centred(x_lat.astype(f32), conv_w, conv_b)
    xc = _conv_centred(x_ctx.astype(f32), conv_w, conv_b)
    outs_l, outs_c = [], []
    for di in range(2):
        rev = di == 1
        prm = (w_a[di], b_a[di], w_x[di], b_x[di], lam[di])
        hc_seq, hc_last = _rglru_direction(_flip(xc, rev), None, *prm)
        hl_seq, _ = _rglru_direction(_flip(xl, rev), hc_last, *prm)
        outs_l.append(_flip(hl_seq, rev))
        outs_c.append(_flip(hc_seq, rev))
    y_ctx = outs_c[0] + outs_c[1] if with_ctx else None
    return outs_l[0] + outs_l[1], y_ctx


def _rotate_half(x, ang):
    x1, x2 = jnp.split(x, 2, axis=-1)
    cos, sin = jnp.cos(ang), jnp.sin(ang)
    return jnp.concatenate([x1 * cos - x2 * sin, x1 * sin + x2 * cos], axis=-1)


def _rope_2d(x, ang_r, ang_c):
    xf = x.astype(jnp.float32)
    xr, xc = jnp.split(xf, 2, axis=-1)
    return jnp.concatenate([_rotate_half(xr, ang_r), _rotate_half(xc, ang_c)], axis=-1).astype(x.dtype)


def _mla_qkv(cq, ckv, kr, q_norm, w_uq, kv_norm, w_ukv, rope):
    bsz, seq_len, _ = cq.shape
    q = (_rmsnorm(cq, q_norm) @ w_uq).reshape(bsz, seq_len, MLA_HEADS, MLA_QK)
    kv = (_rmsnorm(ckv, kv_norm) @ w_ukv).reshape(bsz, seq_len, MLA_HEADS, MLA_NOPE + MLA_V)
    q_nope, q_rope = q[..., :MLA_NOPE], q[..., MLA_NOPE:]
    k_nope, v = kv[..., :MLA_NOPE], kv[..., MLA_NOPE:]
    if rope is not None:
        ang_r, ang_c = rope
        q_rope = _rope_2d(q_rope, ang_r[:, None, :], ang_c[:, None, :])
        kr = _rope_2d(kr, ang_r, ang_c)
    k_rope = jnp.broadcast_to(kr[:, :, None, :], (bsz, seq_len, MLA_HEADS, MLA_ROPE))
    q = jnp.concatenate([q_nope, q_rope.astype(q_nope.dtype)], axis=-1)
    k = jnp.concatenate([k_nope, k_rope.astype(k_nope.dtype)], axis=-1)
    return q, k, v


def _attend(q, k, v):
    s = jnp.einsum('bqhd,bkhd->bhqk', q, k, preferred_element_type=jnp.float32) * ATTN_SCALE
    p = jax.nn.softmax(s, axis=-1)
    return jnp.einsum('bhqk,bkhd->bqhd', p.astype(v.dtype), v)


def _mla_mixer(cq_l, ckv_l, kr_l, cq_c, ckv_c, kr_c, q_norm, w_uq, kv_norm, w_ukv, ang_r, ang_c, with_ctx):
    q_l, k_l, v_l = _mla_qkv(cq_l, ckv_l, kr_l, q_norm, w_uq, kv_norm, w_ukv, (ang_r, ang_c))
    q_c, k_c, v_c = _mla_qkv(cq_c, ckv_c, kr_c, q_norm, w_uq, kv_norm, w_ukv, None)
    k_all = jnp.concatenate([k_c, k_l], axis=1)
    v_all = jnp.concatenate([v_c, v_l], axis=1)
    bsz, seq_len = q_l.shape[0], q_l.shape[1]
    n_blk = seq_len // Q_BLOCK
    qb = jnp.moveaxis(q_l.reshape(bsz, n_blk, Q_BLOCK, MLA_HEADS, MLA_QK), 1, 0)
    ob = lax.map(lambda q_blk: _attend(q_blk, k_all, v_all), qb)
    y_l = jnp.moveaxis(ob, 0, 1).reshape(bsz, seq_len, MLA_WIDTH)
    y_c = _attend(q_c, k_c, v_c).reshape(bsz, q_c.shape[1], MLA_WIDTH) if with_ctx else None
    return y_l, y_c


def _merge(y_s5, y_lru, y_mla, z_s5, z_lru, z_mla, w_out, dtype):
    cat = jnp.concatenate([y_s5 * jax.nn.silu(z_s5), y_lru * jax.nn.silu(z_lru),
                           y_mla * jax.nn.silu(z_mla)], axis=-1)
    return (cat @ w_out).astype(dtype)


def setup_inputs(seed: int = 0) -> dict:
    key = jax.random.key(seed)
    ks = jax.random.split(key, 31)
    f32 = jnp.float32
    nrm = lambda k, shape, s: jax.random.normal(k, shape, f32) * s
    x = nrm(ks[0], (BATCH, SEQ, D_MODEL), 1.0)
    c = nrm(ks[1], (BATCH, D_MODEL), 1.0)
    ctx = nrm(ks[2], (BATCH, CTX_LEN, D_MODEL), 1.0)
    c_ctx = nrm(ks[3], (D_MODEL,), 1.0)
    ada_w = nrm(ks[4], (DEPTH, D_MODEL, 3 * D_MODEL), 0.5 * D_MODEL ** -0.5)
    ada_b = nrm(ks[5], (DEPTH, 3 * D_MODEL), 0.02)
    norm_g = 1.0 + nrm(ks[6], (DEPTH, D_MODEL), 0.02)
    w_in = nrm(ks[7], (DEPTH, D_MODEL, D_IN), D_MODEL ** -0.5)
    s5_lam_re = -0.5 + nrm(ks[8], (DEPTH, 2, S5_GROUPS, S5_STATE), 0.01)
    s5_lam_im = math.pi * jnp.arange(S5_STATE, dtype=f32) + nrm(ks[9], (DEPTH, 2, S5_GROUPS, S5_STATE), 0.01)
    s5_log_dt = jax.random.uniform(ks[10], (DEPTH, 2, S5_GROUPS), f32, math.log(1e-3), math.log(1e-1))
    s5_b_re = nrm(ks[11], (DEPTH, 2, S5_GROUPS, S5_STATE, S5_GROUP), (2.0 * S5_GROUP) ** -0.5)
    s5_b_im = nrm(ks[12], (DEPTH, 2, S5_GROUPS, S5_STATE, S5_GROUP), (2.0 * S5_GROUP) ** -0.5)
    s5_c_re = nrm(ks[13], (DEPTH, 2, S5_GROUPS, S5_GROUP, S5_STATE), S5_STATE ** -0.5)
    s5_c_im = nrm(ks[14], (DEPTH, 2, S5_GROUPS, S5_GROUP, S5_STATE), S5_STATE ** -0.5)
    s5_d = nrm(ks[15], (DEPTH, S5_WIDTH), 1.0)
    s5_glu_w = nrm(ks[16], (DEPTH, S5_WIDTH, 2 * S5_WIDTH), S5_WIDTH ** -0.5)
    s5_glu_b = nrm(ks[17], (DEPTH, 2 * S5_WIDTH), 0.02)
    lru_conv_w = nrm(ks[18], (DEPTH, LRU_CONV, LRU_WIDTH), LRU_CONV ** -0.5)
    lru_conv_b = nrm(ks[19], (DEPTH, LRU_WIDTH), 0.02)
    lru_wa = nrm(ks[20], (DEPTH, 2, LRU_HEADS, LRU_HEAD_W, LRU_HEAD_W), LRU_HEAD_W ** -0.5)
    lru_ba = nrm(ks[21], (DEPTH, 2, LRU_WIDTH), 0.02)
    lru_wx = nrm(ks[22], (DEPTH, 2, LRU_HEADS, LRU_HEAD_W, LRU_HEAD_W), LRU_HEAD_W ** -0.5)
    lru_bx = nrm(ks[23], (DEPTH, 2, LRU_WIDTH), 0.02)
    a_c = jax.random.uniform(ks[24], (DEPTH, 2, LRU_WIDTH), f32, 0.9, 0.999)
    a0 = a_c ** (1.0 / LRU_C)
    lru_lam = jnp.log(a0) - jnp.log1p(-a0)
    mla_q_norm = 1.0 + nrm(ks[25], (DEPTH, MLA_Q_RANK), 0.02)
    mla_w_uq = nrm(ks[26], (DEPTH, MLA_Q_RANK, MLA_HEADS * MLA_QK), MLA_Q_RANK ** -0.5)
    mla_kv_norm = 1.0 + nrm(ks[27], (DEPTH, MLA_KV_RANK), 0.02)
    mla_w_ukv = nrm(ks[28], (DEPTH, MLA_KV_RANK, MLA_HEADS * (MLA_NOPE + MLA_V)), MLA_KV_RANK ** -0.5)
    w_out = nrm(ks[29], (DEPTH, D_MIX, D_MODEL), D_MIX ** -0.5)
    final_g = 1.0 + nrm(ks[30], (D_MODEL,), 0.02)
    return {'x': x, 'c': c, 'ctx': ctx, 'c_ctx': c_ctx, 'ada_w': ada_w, 'ada_b': ada_b,
            'norm_g': norm_g, 'w_in': w_in, 's5_lam_re': s5_lam_re, 's5_lam_im': s5_lam_im,
            's5_log_dt': s5_log_dt, 's5_b_re': s5_b_re, 's5_b_im': s5_b_im, 's5_c_re': s5_c_re,
            's5_c_im': s5_c_im, 's5_d': s5_d, 's5_glu_w': s5_glu_w, 's5_glu_b': s5_glu_b,
            'lru_conv_w': lru_conv_w, 'lru_conv_b': lru_conv_b, 'lru_wa': lru_wa, 'lru_ba': lru_ba,
            'lru_wx': lru_wx, 'lru_bx': lru_bx, 'lru_lam': lru_lam, 'mla_q_norm': mla_q_norm,
            'mla_w_uq': mla_w_uq, 'mla_kv_norm': mla_kv_norm, 'mla_w_ukv': mla_w_ukv,
            'w_out': w_out, 'final_g': final_g}


def reference(x, c, ctx, c_ctx, ada_w, ada_b, norm_g, w_in, s5_lam_re, s5_lam_im, s5_log_dt,
              s5_b_re, s5_b_im, s5_c_re, s5_c_im, s5_d, s5_glu_w, s5_glu_b, lru_conv_w, lru_conv_b,
              lru_wa, lru_ba, lru_wx, lru_bx, lru_lam, mla_q_norm, mla_w_uq, mla_kv_norm, mla_w_ukv,
              w_out, final_g):
    f32 = jnp.float32
    seq_len = x.shape[1]
    n_rows = seq_len // GRID_W
    rows = jnp.repeat(jnp.arange(n_rows, dtype=f32), GRID_W)
    cols = jnp.tile(jnp.arange(GRID_W, dtype=f32), n_rows)
    inv_freq = ROPE_BASE ** (-jnp.arange(0, ROPE_AXIS, 2, dtype=f32) / ROPE_AXIS)
    ang_r = rows[:, None] * inv_freq
    ang_c = cols[:, None] * inv_freq
    sc_lat = jax.nn.silu(c)
    sc_ctx = jax.nn.silu(c_ctx)
    for l in range(DEPTH):
        with_ctx = l < DEPTH - 1
        shift, scale, gate = jnp.split(sc_lat @ ada_w[l] + ada_b[l], 3, axis=-1)
        shift_c, scale_c, gate_c = jnp.split(sc_ctx @ ada_w[l] + ada_b[l], 3, axis=-1)
        h = _rmsnorm(x, norm_g[l]) * (1.0 + scale[:, None, :]) + shift[:, None, :]
        hc = _rmsnorm(ctx, norm_g[l]) * (1.0 + scale_c) + shift_c
        pl = jnp.split(h @ w_in[l], SPLIT_IDX, axis=-1)
        pc = jnp.split(hc @ w_in[l], SPLIT_IDX, axis=-1)
        y_s5, yc_s5 = _s5_mixer(pl[0], pc[0], s5_lam_re[l], s5_lam_im[l], s5_log_dt[l], s5_b_re[l],
                                s5_b_im[l], s5_c_re[l], s5_c_im[l], s5_d[l], s5_glu_w[l], s5_glu_b[l],
                                with_ctx)
        y_lru, yc_lru = _rglru_mixer(pl[2], pc[2], lru_conv_w[l], lru_conv_b[l], lru_wa[l], lru_ba[l],
                                     lru_wx[l], lru_bx[l], lru_lam[l], with_ctx)
        y_mla, yc_mla = _mla_mixer(pl[4], pl[5], pl[6], pc[4], pc[5], pc[6], mla_q_norm[l], mla_w_uq[l],
                                   mla_kv_norm[l], mla_w_ukv[l], ang_r, ang_c, with_ctx)
        o = _merge(y_s5, y_lru, y_mla, pl[1], pl[3], pl[7], w_out[l], x.dtype)
        x = x + gate[:, None, :] * o
        if with_ctx:
            oc = _merge(yc_s5, yc_lru, yc_mla, pc[1], pc[3], pc[7], w_out[l], ctx.dtype)
            ctx = ctx + gate_c * oc
    return _rmsnorm(x, final_g)
```

```python
import functools
import math

import jax
import jax.numpy as jnp
import numpy as np
from jax import lax
from jax.experimental import pallas as pl
from jax.experimental.pallas import tpu as pltpu

F32 = jnp.float32
BF16 = jnp.bfloat16

D_MODEL = 1024
BATCH = 16
SEQ = 2048
DEPTH = 4
GRID_W = 64
CTX_LEN = 256
LT = CTX_LEN + SEQ
S5_WIDTH = 256
S5_GROUP = 16
S5_GROUPS = 16
S5_STATE = 64
S5_NSTATE = S5_GROUPS * S5_STATE
LRU_WIDTH = 256
LRU_HEADS = 8
LRU_HEAD_W = 32
LRU_CONV = 4
LRU_C = 8.0
MLA_HEADS = 8
MLA_NOPE = 64
MLA_ROPE = 32
MLA_QK = 96
MLA_V = 64
MLA_WIDTH = 512
MLA_Q_RANK = 256
MLA_KV_RANK = 128
ATTN_SCALE = MLA_QK ** -0.5
ROPE_AXIS = 16
ROPE_BASE = 10000.0
EPS = 1e-6

HEAD_PAD = 128
QK_PAD = MLA_HEADS * HEAD_PAD
IN_PAD = 2048
MOD_ROWS = 24
ROW_BLK = 256
N_ROW_BLK = LT // ROW_BLK
TL = 64
NCT = CTX_LEN // TL
NLT = SEQ // TL
NCH = NCT + NLT
HALO = 8
CBW = 256

_VMEM_LIMIT = 48 * 1024 * 1024


def _split_bf16(v):
    hi = v.astype(BF16)
    lo = (v - hi.astype(F32)).astype(BF16)
    return hi, lo


def _mod_body(c_ref, w_ref, b_ref, o_ref):
    c = c_ref[...]
    sc = c * jax.nn.sigmoid(c)
    s_hi, s_lo = _split_bf16(sc)
    w_hi, w_lo = _split_bf16(w_ref[0])
    acc = jnp.dot(s_hi, w_hi, preferred_element_type=F32)
    acc += jnp.dot(s_hi, w_lo, preferred_element_type=F32)
    acc += jnp.dot(s_lo, w_hi, preferred_element_type=F32)
    o_ref[0] = acc + b_ref[0]


def _mod_tables(c_pad, ada_w, ada_b):
    nj = 3 * D_MODEL // 1024
    return pl.pallas_call(
        _mod_body,
        grid=(DEPTH, nj),
        in_specs=[
            pl.BlockSpec((MOD_ROWS, D_MODEL), lambda l, j: (0, 0)),
            pl.BlockSpec((1, D_MODEL, 1024), lambda l, j: (l, 0, j)),
            pl.BlockSpec((1, 1, 1024), lambda l, j: (l, 0, j)),
        ],
        out_specs=pl.BlockSpec((1, MOD_ROWS, 1024), lambda l, j: (l, 0, j)),
        out_shape=jax.ShapeDtypeStruct((DEPTH, MOD_ROWS, 3 * D_MODEL), F32),
        compiler_params=pltpu.CompilerParams(vmem_limit_bytes=_VMEM_LIMIT),
        name="adaln_tables",
    )(c_pad, ada_w, ada_b.reshape(DEPTH, 1, 3 * D_MODEL))


def _mod_row(mod_ref, b, is_ctx):
    row = jnp.where(is_ctx, BATCH, b)
    return mod_ref[pl.ds(row, 1), :]


def _rms(v, g):
    ms = jnp.mean(v * v, axis=-1, keepdims=True)
    return v * lax.rsqrt(ms + EPS) * g


def _in_body(x_ref, mod_ref, g_ref, w_ref,
             us5_ref, zs5_ref, xlru_ref, zlru_ref, cq_ref, kvr_ref, zmla_ref):
    b = pl.program_id(0)
    t = pl.program_id(1)
    mod = _mod_row(mod_ref, b, t == 0)
    shift = mod[:, 0:D_MODEL]
    scale = mod[:, D_MODEL:2 * D_MODEL]
    h = _rms(x_ref[0], g_ref[...]) * (1.0 + scale) + shift
    res = jnp.dot(h.astype(BF16), w_ref[...], preferred_element_type=F32)
    us5_ref[0] = res[:, 0:256]
    zs5_ref[0] = res[:, 256:512]
    xlru_ref[0] = res[:, 512:768]
    zlru_ref[0] = res[:, 768:1024]
    cq_ref[0] = res[:, 1024:1280]
    kvr_ref[0] = res[:, 1280:1536]
    zmla_ref[0] = res[:, 1536:2048]


def _in_proj(x_all, mod_l, norm_g, w_in_pad):
    row_spec = lambda w: pl.BlockSpec((1, ROW_BLK, w), lambda b, t: (b, t, 0))
    out = lambda w: jax.ShapeDtypeStruct((BATCH, LT, w), F32)
    return pl.pallas_call(
        _in_body,
        grid=(BATCH, N_ROW_BLK),
        in_specs=[
            row_spec(D_MODEL),
            pl.BlockSpec((MOD_ROWS, 3 * D_MODEL), lambda b, t: (0, 0)),
            pl.BlockSpec((1, D_MODEL), lambda b, t: (0, 0)),
            pl.BlockSpec((D_MODEL, IN_PAD), lambda b, t: (0, 0)),
        ],
        out_specs=[row_spec(256), row_spec(256), row_spec(256), row_spec(256),
                   row_spec(256), row_spec(256), row_spec(512)],
        out_shape=[out(256), out(256), out(256), out(256), out(256), out(256), out(512)],
        compiler_params=pltpu.CompilerParams(vmem_limit_bytes=_VMEM_LIMIT),
        name="in_proj",
    )(x_all, mod_l, norm_g, w_in_pad)


def _chunk_index(d, i):
    bwd = jnp.where(i < NCT, NCT - 1 - i, 2 * NCT + NLT - 1 - i)
    return jnp.where(d == 0, i, bwd)


def _scan_body(us5_ref, xl_ref, xp_ref, xn_ref, bb_ref, cc_ref, ab_ref, cw_ref, cb_ref,
               wg_ref, bg_ref, sp_ref, ys5_ref, ylru_ref,
               utm, xs, ytm, xpad, la, lb, hs5, hlru):
    d = pl.program_id(0)
    i = pl.program_id(1)
    c = _chunk_index(d, i)
    rows = TL * BATCH

    @pl.when(i == 0)
    def _():
        hs5[...] = jnp.zeros_like(hs5)
        hlru[...] = jnp.zeros_like(hlru)

    prev_ok = jnp.where((c != 0) & (c != NCT), 1.0, 0.0).astype(F32)
    next_ok = jnp.where((c != NCT - 1) & (c != NCH - 1), 1.0, 0.0).astype(F32)
    for b in range(BATCH):
        for k in range(2):
            ks = slice(k * 128, (k + 1) * 128)
            utm[k, pl.ds(b, TL, stride=BATCH), :] = us5_ref[b, :, ks]
            xpad[k, pl.ds(2 * BATCH + b, TL, stride=BATCH), :] = xl_ref[b, :, ks]
            xpad[k, pl.ds(b, 2, stride=BATCH), :] = xp_ref[b, HALO - 2:HALO, ks] * prev_ok
            xpad[k, pl.ds((TL + 2) * BATCH + b, 1), :] = xn_ref[b, 0:1, ks] * next_ok

    def row_of(t):
        te = t + d * (TL - 1 - 2 * t)
        return pl.multiple_of(te * BATCH, BATCH)

    def both(ref, rs):
        return jnp.concatenate([ref[0, rs, :], ref[1, rs, :]], axis=-1)

    def bmat(r, carry):
        rs = pl.ds(pl.multiple_of(r * 256, 256), 256)
        xs[rs, :] = jnp.dot(both(utm, rs).astype(BF16), bb_ref[0], preferred_element_type=F32)
        return carry
    lax.fori_loop(0, rows // 256, bmat, 0)

    for cb in range(S5_NSTATE // CBW):
        lo = cb * CBW
        re = slice(lo, lo + CBW)
        im = slice(S5_NSTATE + lo, S5_NSTATE + lo + CBW)
        ar = jnp.broadcast_to(ab_ref[0, 0:1, re], (BATCH, CBW))
        ai = jnp.broadcast_to(ab_ref[0, 1:2, re], (BATCH, CBW))

        def s5_step(t, carry, re=re, im=im, ar=ar, ai=ai):
            hr, hi = carry
            r = row_of(t)
            nr = ar * hr - ai * hi + xs[pl.ds(r, BATCH), re]
            ni = ar * hi + ai * hr + xs[pl.ds(r, BATCH), im]
            xs[pl.ds(r, BATCH), re] = nr
            xs[pl.ds(r, BATCH), im] = ni
            return nr, ni

        hr, hi = lax.fori_loop(0, TL, s5_step, (hs5[0, :, re], hs5[1, :, re]), unroll=4)
        hs5[0, :, re] = hr
        hs5[1, :, re] = hi

    def cmat(r, carry):
        rs = pl.ds(pl.multiple_of(r * 256, 256), 256)
        y = jnp.dot(xs[rs, :].astype(BF16), cc_ref[0], preferred_element_type=F32)
        ytm[0, rs, :] = y[:, 0:128]
        ytm[1, rs, :] = y[:, 128:256]
        return carry
    lax.fori_loop(0, rows // 256, cmat, 0)

    def gates(r, carry):
        r0 = pl.multiple_of(r * 256, 256)
        rs = pl.ds(r0, 256)
        xc = cb_ref[...]
        for k in range(LRU_CONV):
            xc = xc + both(xpad, pl.ds(r0 + k * BATCH, 256)) * cw_ref[k:k + 1, :]
        g = jnp.dot(xc.astype(BF16), wg_ref[0], preferred_element_type=F32) + bg_ref[0]
        rg = jax.nn.sigmoid(g[:, 0:LRU_WIDTH])
        ig = jax.nn.sigmoid(g[:, LRU_WIDTH:2 * LRU_WIDTH])
        log_a = (-LRU_C * rg) * sp_ref[0]
        a = jnp.exp(log_a)
        one_m_a2 = jnp.tanh(-log_a) * (a * a + 1.0)
        mult = jnp.sqrt(jnp.maximum(one_m_a2, 0.0))
        bv = mult * (ig * xc)
        for k in range(2):
            la[k, rs, :] = a[:, k * 128:(k + 1) * 128]
            lb[k, rs, :] = bv[:, k * 128:(k + 1) * 128]
        return carry
    lax.fori_loop(0, rows // 256, gates, 0)

    def lru_step(t, h):
        rs = pl.ds(row_of(t), BATCH)
        nh = tuple(la[k, rs, :] * h[k] + lb[k, rs, :] for k in range(2))
        for k in range(2):
            lb[k, rs, :] = nh[k]
        return nh
    h_fin = lax.fori_loop(0, TL, lru_step, (hlru[0], hlru[1]), unroll=4)
    hlru[0] = h_fin[0]
    hlru[1] = h_fin[1]

    for b in range(BATCH):
        for k in range(2):
            ks = slice(k * 128, (k + 1) * 128)
            ys5_ref[0, b, :, ks] = ytm[k, pl.ds(b, TL, stride=BATCH), :]
            ylru_ref[0, b, :, ks] = lb[k, pl.ds(b, TL, stride=BATCH), :]


def _scan(us5, xlru, bb, cc, ab, conv_w, conv_b, wg, bg, sp):
    n_halo = LT // HALO
    chunk = lambda d, i: (0, _chunk_index(d, i), 0)
    prev = lambda d, i: (0, jnp.maximum(_chunk_index(d, i) * (TL // HALO) - 1, 0), 0)
    nxt = lambda d, i: (0, jnp.minimum((_chunk_index(d, i) + 1) * (TL // HALO), n_halo - 1), 0)
    per_dir = lambda s: pl.BlockSpec((1,) + s, lambda d, i: (d,) + (0,) * len(s))
    const = lambda s: pl.BlockSpec(s, lambda d, i: (0,) * len(s))
    out_spec = pl.BlockSpec((1, BATCH, TL, 256), lambda d, i: (d, 0, _chunk_index(d, i), 0))
    rows = TL * BATCH
    return pl.pallas_call(
        _scan_body,
        grid=(2, NCH),
        in_specs=[
            pl.BlockSpec((BATCH, TL, 256), chunk),
            pl.BlockSpec((BATCH, TL, 256), chunk),
            pl.BlockSpec((BATCH, HALO, 256), prev),
            pl.BlockSpec((BATCH, HALO, 256), nxt),
            per_dir((S5_WIDTH, 2 * S5_NSTATE)),
            per_dir((2 * S5_NSTATE, S5_WIDTH)),
            per_dir((2, S5_NSTATE)),
            const((LRU_CONV, LRU_WIDTH)),
            const((1, LRU_WIDTH)),
            per_dir((LRU_WIDTH, 2 * LRU_WIDTH)),
            per_dir((1, 2 * LRU_WIDTH)),
            per_dir((1, LRU_WIDTH)),
        ],
        out_specs=[out_spec, out_spec],
        out_shape=[jax.ShapeDtypeStruct((2, BATCH, LT, 256), F32)] * 2,
        scratch_shapes=[
            pltpu.VMEM((2, rows, 128), F32),
            pltpu.VMEM((rows, 2 * S5_NSTATE), F32),
            pltpu.VMEM((2, rows, 128), F32),
            pltpu.VMEM((2, (TL + 3) * BATCH, 128), F32),
            pltpu.VMEM((2, rows, 128), F32),
            pltpu.VMEM((2, rows, 128), F32),
            pltpu.VMEM((2, BATCH, S5_NSTATE), F32),
            pltpu.VMEM((2, BATCH, 128), F32),
        ],
        compiler_params=pltpu.CompilerParams(vmem_limit_bytes=_VMEM_LIMIT),
        name="s5_lru_scan",
    )(us5, xlru, xlru, xlru, bb, cc, ab, conv_w, conv_b, wg, bg, sp)


def _rope(v, cos, sin):
    width = v.shape[-1]
    lane = lax.broadcasted_iota(jnp.int32, v.shape, 1)
    first_half = ((lane // 8) % 2) == 0
    swapped = jnp.where(first_half, pltpu.roll(v, width - 8, 1), pltpu.roll(v, 8, 1))
    return v * cos + swapped * sin


def _attn_body(cq_ref, kvr_ref, cosq_ref, sinq_ref, cosk_ref, sink_ref, qn_ref, wuq_ref,
               kvn_ref, wukv_ref, o_ref, k_s, v_s, *, q_off):
    j = pl.program_id(1)

    @pl.when(j == 0)
    def _prep():
        def blk(r, carry):
            r0 = pl.multiple_of(r * ROW_BLK, ROW_BLK)
            kvr = kvr_ref[0, pl.ds(r0, ROW_BLK), :]
            kvn = _rms(kvr[:, 0:MLA_KV_RANK], kvn_ref[...])
            kv = jnp.dot(kvn.astype(BF16), wukv_ref[...], preferred_element_type=F32)
            kro = _rope(kvr[:, MLA_KV_RANK:], cosk_ref[pl.ds(r0, ROW_BLK), :],
                        sink_ref[pl.ds(r0, ROW_BLK), :])
            for h in range(MLA_HEADS):
                hs = slice(h * HEAD_PAD, (h + 1) * HEAD_PAD)
                k_s[pl.ds(r0, ROW_BLK), hs] = (kv[:, hs] + kro).astype(BF16)
            v_s[pl.ds(r0, ROW_BLK), :] = kv[:, QK_PAD:].astype(BF16)
            return carry
        lax.fori_loop(0, N_ROW_BLK, blk, 0)

    qn = _rms(cq_ref[0], qn_ref[...])
    q = jnp.dot(qn.astype(BF16), wuq_ref[...], preferred_element_type=F32)
    cosq = cosq_ref[...]
    sinq = sinq_ref[...]
    lane = lax.broadcasted_iota(jnp.int32, (ROW_BLK, 2 * MLA_V), 1)

    def attend(nk):
        for hp in range(MLA_HEADS // 2):
            pair = []
            for hh in range(2):
                h = 2 * hp + hh
                hs = slice(h * HEAD_PAD, (h + 1) * HEAD_PAD)
                qh = _rope(q[:, hs], cosq, sinq).astype(BF16)
                s = lax.dot_general(qh, k_s[0:nk, hs], (((1,), (1,)), ((), ())),
                                    preferred_element_type=F32)
                m = jnp.max(s, axis=-1, keepdims=True)
                p = jnp.exp(s - m)
                l = jnp.sum(p, axis=-1, keepdims=True)
                pv = jnp.dot(p.astype(BF16), v_s[0:nk, hp * 128:(hp + 1) * 128],
                             preferred_element_type=F32)
                pair.append(pv / l)
            o_ref[0, :, hp * 128:(hp + 1) * 128] = jnp.where(lane < MLA_V, pair[0], pair[1])

    is_ctx = (j + q_off) == 0

    @pl.when(is_ctx)
    def _():
        attend(CTX_LEN)

    @pl.when(jnp.logical_not(is_ctx))
    def _():
        attend(LT)


def _attention(cq, kvr, cosq, sinq, cosk, sink, q_norm, w_uq_pad, kv_norm, w_ukv_pad, with_ctx):
    q_off = 0 if with_ctx else 1
    const = lambda s: pl.BlockSpec(s, lambda b, j: (0,) * len(s))
    return pl.pallas_call(
        functools.partial(_attn_body, q_off=q_off),
        grid=(BATCH, N_ROW_BLK - q_off),
        in_specs=[
            pl.BlockSpec((1, ROW_BLK, MLA_Q_RANK), lambda b, j: (b, j + q_off, 0)),
            pl.BlockSpec((1, LT, 256), lambda b, j: (b, 0, 0)),
            pl.BlockSpec((ROW_BLK, HEAD_PAD), lambda b, j: (j + q_off, 0)),
            pl.BlockSpec((ROW_BLK, HEAD_PAD), lambda b, j: (j + q_off, 0)),
            const((LT, HEAD_PAD)),
            const((LT, HEAD_PAD)),
            const((1, MLA_Q_RANK)),
            const((MLA_Q_RANK, QK_PAD)),
            const((1, MLA_KV_RANK)),
            const((MLA_KV_RANK, QK_PAD + MLA_WIDTH)),
        ],
        out_specs=pl.BlockSpec((1, ROW_BLK, MLA_WIDTH), lambda b, j: (b, j + q_off, 0)),
        out_shape=jax.ShapeDtypeStruct((BATCH, LT, MLA_WIDTH), F32),
        scratch_shapes=[
            pltpu.VMEM((LT, QK_PAD), BF16),
            pltpu.VMEM((LT, MLA_WIDTH), BF16),
        ],
        compiler_params=pltpu.CompilerParams(vmem_limit_bytes=_VMEM_LIMIT),
        name="mla_attention",
    )(cq, kvr, cosq, sinq, cosk, sink, q_norm, w_uq_pad, kv_norm, w_ukv_pad)


def _silu(v):
    return v * jax.nn.sigmoid(v)


def _out_body(x_ref, mod_ref, ys5_ref, us5_ref, zs5_ref, ylru_ref, zlru_ref, ymla_ref, zmla_ref,
              d_ref, gw_ref, gb_ref, wo_ref, fg_ref, o_ref, *, t_off, final):
    b = pl.program_id(0)
    t = pl.program_id(1) + t_off
    gate = _mod_row(mod_ref, b, t == 0)[:, 2 * D_MODEL:3 * D_MODEL]
    y = d_ref[...] * us5_ref[0] + ys5_ref[0, 0] + ys5_ref[1, 0]
    g = jnp.dot(y.astype(BF16), gw_ref[...], preferred_element_type=F32) + gb_ref[...]
    y_s5 = g[:, 0:S5_WIDTH] * jax.nn.sigmoid(g[:, S5_WIDTH:])
    y_lru = ylru_ref[0, 0] + ylru_ref[1, 0]
    cat = jnp.concatenate([
        (y_s5 * _silu(zs5_ref[0])).astype(BF16),
        (y_lru * _silu(zlru_ref[0])).astype(BF16),
        (ymla_ref[0] * _silu(zmla_ref[0])).astype(BF16)], axis=-1)
    o = jnp.dot(cat, wo_ref[...], preferred_element_type=F32)
    xn = x_ref[0] + gate * o
    if final:
        xn = _rms(xn, fg_ref[...])
    o_ref[0] = xn


def _out_proj(x_all, mod_l, ys5, us5, zs5, ylru, zlru, ymla, zmla, s5_d, glu_w, glu_b, w_out,
              final_g, final):
    t_off = 1 if final else 0
    row_spec = lambda w: pl.BlockSpec((1, ROW_BLK, w), lambda b, t: (b, t + t_off, 0))
    dir_spec = pl.BlockSpec((2, 1, ROW_BLK, 256), lambda b, t: (0, b, t + t_off, 0))
    const = lambda s: pl.BlockSpec(s, lambda b, t: (0,) * len(s))
    if final:
        out_shape = jax.ShapeDtypeStruct((BATCH, SEQ, D_MODEL), F32)
        out_spec = pl.BlockSpec((1, ROW_BLK, D_MODEL), lambda b, t: (b, t, 0))
        aliases = {}
    else:
        out_shape = jax.ShapeDtypeStruct((BATCH, LT, D_MODEL), F32)
        out_spec = row_spec(D_MODEL)
        aliases = {0: 0}
    return pl.pallas_call(
        functools.partial(_out_body, t_off=t_off, final=final),
        grid=(BATCH, N_ROW_BLK - t_off),
        in_specs=[
            row_spec(D_MODEL),
            const((MOD_ROWS, 3 * D_MODEL)),
            dir_spec, row_spec(256), row_spec(256),
            dir_spec, row_spec(256),
            row_spec(512), row_spec(512),
            const((1, S5_WIDTH)),
            const((S5_WIDTH, 2 * S5_WIDTH)),
            const((1, 2 * S5_WIDTH)),
            const((D_MODEL, D_MODEL)),
            const((1, D_MODEL)),
        ],
        out_specs=out_spec,
        out_shape=out_shape,
        input_output_aliases=aliases,
        compiler_params=pltpu.CompilerParams(vmem_limit_bytes=_VMEM_LIMIT),
        name="merge_out_proj",
    )(x_all, mod_l, ys5, us5, zs5, ylru, zlru, ymla, zmla, s5_d, glu_w, glu_b, w_out, final_g)


def _block_diag(blocks):
    n, r, c = blocks.shape
    eye = jnp.eye(n, dtype=blocks.dtype)
    return (eye[:, None, :, None] * blocks[:, :, None, :]).reshape(n * r, n * c)


def _s5_matrices(lam_re, lam_im, log_dt, b_re, b_im, c_re, c_im):
    dt = jnp.exp(log_dt)[:, None]
    mag = jnp.exp(lam_re * dt)
    ab_re, ab_im = mag * jnp.cos(lam_im * dt), mag * jnp.sin(lam_im * dt)
    den = lam_re * lam_re + lam_im * lam_im
    nr = ab_re - 1.0
    coef_re = (nr * lam_re + ab_im * lam_im) / den
    coef_im = (ab_im * lam_re - nr * lam_im) / den
    bb_re = coef_re[..., None] * b_re - coef_im[..., None] * b_im
    bb_im = coef_re[..., None] * b_im + coef_im[..., None] * b_re
    to_in = lambda m: _block_diag(jnp.swapaxes(m, 1, 2))
    bb = jnp.concatenate([to_in(bb_re), to_in(bb_im)], axis=1)
    to_out = lambda m: _block_diag(jnp.swapaxes(m, 1, 2))
    cc = jnp.concatenate([to_out(c_re), -to_out(c_im)], axis=0)
    ab = jnp.stack([ab_re.reshape(-1), ab_im.reshape(-1)])
    return bb.astype(BF16), cc.astype(BF16), ab


def _rope_tables():
    n_rows = SEQ // GRID_W
    rows = jnp.repeat(jnp.arange(n_rows, dtype=F32), GRID_W)
    cols = jnp.tile(jnp.arange(GRID_W, dtype=F32), n_rows)
    inv_freq = ROPE_BASE ** (-jnp.arange(0, ROPE_AXIS, 2, dtype=F32) / ROPE_AXIS)
    ang_r = rows[:, None] * inv_freq
    ang_c = cols[:, None] * inv_freq
    cr, sr, cc, sc = jnp.cos(ang_r), jnp.sin(ang_r), jnp.cos(ang_c), jnp.sin(ang_c)
    cos32 = jnp.concatenate([cr, cr, cc, cc], axis=-1)
    sin32 = jnp.concatenate([-sr, sr, -sc, sc], axis=-1)
    cos = jnp.ones((LT, HEAD_PAD), F32).at[CTX_LEN:, MLA_NOPE:MLA_QK].set(cos32)
    sin = jnp.zeros((LT, HEAD_PAD), F32).at[CTX_LEN:, MLA_NOPE:MLA_QK].set(sin32)
    return cos, sin


def kernel(x, c, ctx, c_ctx, ada_w, ada_b, norm_g, w_in, s5_lam_re, s5_lam_im, s5_log_dt, s5_b_re,
           s5_b_im, s5_c_re, s5_c_im, s5_d, s5_glu_w, s5_glu_b, lru_conv_w, lru_conv_b, lru_wa,
           lru_ba, lru_wx, lru_bx, lru_lam, mla_q_norm, mla_w_uq, mla_kv_norm, mla_w_ukv, w_out,
           final_g):
    x_all = jnp.concatenate([ctx, x], axis=1)
    c_pad = jnp.zeros((MOD_ROWS, D_MODEL), F32).at[:BATCH].set(c).at[BATCH].set(c_ctx)
    mod = _mod_tables(c_pad, ada_w, ada_b)

    cosk, sink = _rope_tables()
    cosq, sinq = cosk * ATTN_SCALE, sink * ATTN_SCALE

    out = None
    for l in range(DEPTH):
        final = l == DEPTH - 1
        wi = w_in[l]
        kr_pad = jnp.zeros((D_MODEL, HEAD_PAD), F32).at[:, MLA_NOPE:MLA_QK].set(wi[:, 1408:1440])
        w_in_pad = jnp.concatenate([wi[:, 0:1408], kr_pad, wi[:, 1440:1952]], axis=1).astype(BF16)
        us5, zs5, xlru, zlru, cq, kvr, zmla = _in_proj(x_all, mod[l], norm_g[l][None], w_in_pad)

        mats = [_s5_matrices(s5_lam_re[l, di], s5_lam_im[l, di], s5_log_dt[l, di], s5_b_re[l, di],
                             s5_b_im[l, di], s5_c_re[l, di], s5_c_im[l, di]) for di in range(2)]
        bb = jnp.stack([m[0] for m in mats])
        cc = jnp.stack([m[1] for m in mats])
        ab = jnp.stack([m[2] for m in mats])
        wg = jnp.stack([jnp.concatenate([_block_diag(lru_wa[l, di]), _block_diag(lru_wx[l, di])],
                                        axis=1) for di in range(2)]).astype(BF16)
        bg = jnp.concatenate([lru_ba[l], lru_bx[l]], axis=-1)[:, None, :]
        sp = jax.nn.softplus(-lru_lam[l])[:, None, :]
        ys5, ylru = _scan(us5, xlru, bb, cc, ab, lru_conv_w[l], lru_conv_b[l][None], wg, bg, sp)

        w_uq_pad = jnp.pad(mla_w_uq[l].reshape(MLA_Q_RANK, MLA_HEADS, MLA_QK),
                           ((0, 0), (0, 0), (0, HEAD_PAD - MLA_QK))).reshape(MLA_Q_RANK, QK_PAD)
        ukv = mla_w_ukv[l].reshape(MLA_KV_RANK, MLA_HEADS, MLA_NOPE + MLA_V)
        w_uk_pad = jnp.pad(ukv[..., :MLA_NOPE], ((0, 0), (0, 0), (0, HEAD_PAD - MLA_NOPE)))
        w_ukv_pad = jnp.concatenate([w_uk_pad.reshape(MLA_KV_RANK, QK_PAD),
                                     ukv[..., MLA_NOPE:].reshape(MLA_KV_RANK, MLA_WIDTH)], axis=1)
        ymla = _attention(cq, kvr, cosq, sinq, cosk, sink, mla_q_norm[l][None],
                          w_uq_pad.astype(BF16), mla_kv_norm[l][None], w_ukv_pad.astype(BF16),
                          with_ctx=not final)

        res = _out_proj(x_all, mod[l], ys5, us5, zs5, ylru, zlru, ymla, zmla, s5_d[l][None],
                        s5_glu_w[l].astype(BF16), s5_glu_b[l][None], w_out[l].astype(BF16),
                        final_g[None], final)
        if final:
            out = res
        else:
            x_all = res
    return out
```

```python
import functools
import math

import jax
import jax.numpy as jnp
from jax import lax
from jax.experimental import pallas as pl
from jax.experimental.pallas import tpu as pltpu

F32 = jnp.float32
BF16 = jnp.bfloat16

D_MODEL = 1024
BATCH = 16
SEQ = 2048
DEPTH = 4
GRID_W = 64
CTX_LEN = 256
LT = CTX_LEN + SEQ
S5_WIDTH = 256
S5_GROUPS = 16
S5_STATE = 64
S5_NSTATE = S5_GROUPS * S5_STATE
LRU_WIDTH = 256
LRU_CONV = 4
LRU_C = 8.0
MLA_HEADS = 8
MLA_NOPE = 64
MLA_QK = 96
MLA_V = 64
MLA_WIDTH = 512
MLA_Q_RANK = 256
MLA_KV_RANK = 128
ATTN_SCALE = MLA_QK ** -0.5
ROPE_AXIS = 16
ROPE_BASE = 10000.0
EPS = 1e-6

LANES = 128
HEAD_PAD = LANES
QK_PAD = MLA_HEADS * HEAD_PAD
IN_PAD = 2048
MOD_ROWS = 24
ROW_BLK = 256
N_ROW_BLK = LT // ROW_BLK
TL = 64
NCT = CTX_LEN // TL
NLT = SEQ // TL
NCH = NCT + NLT
HALO = 16
CBW = 512
MXU_TILE = 256

_VMEM_LIMIT = 48 * 1024 * 1024


def _split_bf16(v):
    hi = v.astype(BF16)
    lo = (v - hi.astype(F32)).astype(BF16)
    return hi, lo


def _sigmoid(v):
    return 0.5 * jnp.tanh(0.5 * v) + 0.5


def _silu(v):
    return v * _sigmoid(v)


def _rms(v, g):
    ms = jnp.mean(v * v, axis=-1, keepdims=True)
    return v * lax.rsqrt(ms + EPS) * g


def _layer_spec(l, shape):
    zeros = (0,) * len(shape)
    return pl.BlockSpec((None,) + shape, lambda a, b: (l,) + zeros)


def _mod_body(c_ref, w_ref, b_ref, o_ref):
    c = c_ref[...]
    sc = c * jax.nn.sigmoid(c)
    s_hi, s_lo = _split_bf16(sc)
    w_hi, w_lo = _split_bf16(w_ref[0])
    acc = jnp.dot(s_hi, w_hi, preferred_element_type=F32)
    acc += jnp.dot(s_hi, w_lo, preferred_element_type=F32)
    acc += jnp.dot(s_lo, w_hi, preferred_element_type=F32)
    o_ref[0] = acc + b_ref[0]


def _mod_tables(c_pad, ada_w, ada_b):
    nj = 3 * D_MODEL // 1024
    return pl.pallas_call(
        _mod_body,
        grid=(DEPTH, nj),
        in_specs=[
            pl.BlockSpec((MOD_ROWS, D_MODEL), lambda l, j: (0, 0)),
            pl.BlockSpec((1, D_MODEL, 1024), lambda l, j: (l, 0, j)),
            pl.BlockSpec((1, 1, 1024), lambda l, j: (l, 0, j)),
        ],
        out_specs=pl.BlockSpec((1, MOD_ROWS, 1024), lambda l, j: (l, 0, j)),
        out_shape=jax.ShapeDtypeStruct((DEPTH, MOD_ROWS, 3 * D_MODEL), F32),
        compiler_params=pltpu.CompilerParams(vmem_limit_bytes=_VMEM_LIMIT),
        name="adaln_tables",
    )(c_pad, ada_w, ada_b.reshape(DEPTH, 1, 3 * D_MODEL))


PROJ_US5, PROJ_XLRU, PROJ_ZLRU, PROJ_CQ, PROJ_KVR = 0, 2, 3, 4, 5


def _row_body(*refs, layer, merge, project, final, split_src):
    it = iter(refs)
    ctx_ref = next(it) if split_src else None
    x_ref, mod_ref = next(it), next(it)
    if merge:
        ysc_ref, uz_ref, zlru_ref, ymla_ref, zmla_ref, d_ref, gw_ref, gb_ref, wo_ref = (
            next(it) for _ in range(9))
    fg_ref = next(it) if final else None
    if project:
        g_ref, wi_ref = next(it), next(it)
    outs = list(it)

    b = pl.program_id(0)
    t = pl.program_id(1) + (1 if final else 0)
    is_ctx = t == 0
    mod_row = jnp.where(is_ctx, BATCH, b)
    x = x_ref[0]
    if split_src:
        x = jnp.where(is_ctx, ctx_ref[0], x)

    if merge:
        gate = mod_ref[layer, pl.ds(mod_row, 1), 2 * D_MODEL:3 * D_MODEL]
        ysc = ysc_ref[0, 0].astype(F32) + ysc_ref[1, 0].astype(F32)
        uz = uz_ref[0].astype(F32)
        y = d_ref[...] * uz[:, 0:S5_WIDTH] + ysc[:, 0:S5_WIDTH]
        g = jnp.dot(y.astype(BF16), gw_ref[...], preferred_element_type=F32) + gb_ref[...]
        y_s5 = g[:, 0:S5_WIDTH] * _sigmoid(g[:, S5_WIDTH:])
        cat = jnp.concatenate([
            (y_s5 * _silu(uz[:, S5_WIDTH:])).astype(BF16),
            (ysc[:, S5_WIDTH:] * _silu(zlru_ref[0].astype(F32))).astype(BF16),
            (ymla_ref[0].astype(F32) * _silu(zmla_ref[0].astype(F32))).astype(BF16)], axis=-1)
        x = x + gate * jnp.dot(cat, wo_ref[...], preferred_element_type=F32)
        if final:
            outs.pop(0)[0] = _rms(x, fg_ref[...])
        else:
            outs.pop(0)[0] = x

    if project:
        nxt = layer + 1 if merge else layer
        shift = mod_ref[nxt, pl.ds(mod_row, 1), 0:D_MODEL]
        scale = mod_ref[nxt, pl.ds(mod_row, 1), D_MODEL:2 * D_MODEL]
        h = _rms(x, g_ref[...]) * (1.0 + scale) + shift
        outs.pop(0)[0] = jnp.dot(h.astype(BF16), wi_ref[...],
                                 preferred_element_type=F32).astype(BF16)


def _row_stage(layer, x_src, mod, merge_in, proj_w, final_g, *, merge, project, final=False):
    split_src = len(x_src) == 2
    t_off = 1 if final else 0
    row_spec = lambda w, cb=0: pl.BlockSpec((1, ROW_BLK, w), lambda b, t: (b, t + t_off, cb))
    const = lambda s: pl.BlockSpec(s, lambda b, t: (0,) * len(s))
    args, in_specs = [], []
    if split_src:
        args += list(x_src)
        in_specs += [pl.BlockSpec((1, ROW_BLK, D_MODEL), lambda b, t: (b, 0, 0)),
                     pl.BlockSpec((1, ROW_BLK, D_MODEL), lambda b, t: (b, jnp.maximum(t - 1, 0), 0))]
    else:
        args += list(x_src)
        in_specs += [row_spec(D_MODEL)]
    args.append(mod)
    in_specs.append(const((DEPTH, MOD_ROWS, 3 * D_MODEL)))
    if merge:
        yscan, proj, ymla, s5_d, glu_w, glu_b, w_out = merge_in
        args += [yscan, proj, proj, ymla, proj, s5_d, glu_w, glu_b, w_out]
        in_specs += [
            pl.BlockSpec((2, 1, ROW_BLK, 512), lambda b, t: (0, b, t + t_off, 0)),
            row_spec(512, 0),
            row_spec(256, PROJ_ZLRU),
            row_spec(MLA_WIDTH),
            row_spec(512, 3),
            _layer_spec(layer, (1, S5_WIDTH)),
            _layer_spec(layer, (S5_WIDTH, 2 * S5_WIDTH)),
            _layer_spec(layer, (1, 2 * S5_WIDTH)),
            _layer_spec(layer, (D_MODEL, D_MODEL)),
        ]
    if final:
        args.append(final_g)
        in_specs.append(const((1, D_MODEL)))
    if project:
        nxt = layer + 1 if merge else layer
        norm_g, w_in_pad = proj_w
        args += [norm_g, w_in_pad]
        in_specs += [_layer_spec(nxt, (1, D_MODEL)), _layer_spec(nxt, (D_MODEL, IN_PAD))]

    out_shape, out_specs, aliases = [], [], {}
    if merge:
        if final:
            out_shape.append(jax.ShapeDtypeStruct((BATCH, SEQ, D_MODEL), F32))
            out_specs.append(pl.BlockSpec((1, ROW_BLK, D_MODEL), lambda b, t: (b, t, 0)))
        else:
            out_shape.append(jax.ShapeDtypeStruct((BATCH, LT, D_MODEL), F32))
            out_specs.append(row_spec(D_MODEL))
            if not split_src:
                aliases = {0: 0}
    if project:
        out_shape.append(jax.ShapeDtypeStruct((BATCH, LT, IN_PAD), BF16))
        out_specs.append(row_spec(IN_PAD))
    return pl.pallas_call(
        functools.partial(_row_body, layer=layer, merge=merge, project=project, final=final,
                          split_src=split_src),
        grid=(BATCH, N_ROW_BLK - t_off),
        in_specs=in_specs,
        out_specs=out_specs,
        out_shape=out_shape,
        input_output_aliases=aliases,
        compiler_params=pltpu.CompilerParams(vmem_limit_bytes=_VMEM_LIMIT),
        name="row_stage",
    )(*args)


def _chunk_index(d, i):
    bwd = jnp.where(i < NCT, NCT - 1 - i, 2 * NCT + NLT - 1 - i)
    return jnp.where(d == 0, i, bwd)


def _time_major(v):
    t = v.shape[1]
    return jnp.swapaxes(v, 0, 1).reshape(t * BATCH, v.shape[2])


def _batch_major(v):
    return jnp.swapaxes(v.reshape(v.shape[0] // BATCH, BATCH, v.shape[1]), 0, 1)


def _scan_body(us5_ref, xl_ref, xp_ref, xn_ref, bb_ref, cc_ref, ab_ref, cw_ref, cb_ref,
               wg_ref, bg_ref, sp_ref, ysc_ref,
               xs, hb, xpad, la, lb, hs5, hlru):
    d = pl.program_id(0)
    i = pl.program_id(1)
    c = _chunk_index(d, i)
    rows = TL * BATCH

    @pl.when(i == 0)
    def _():
        hs5[...] = jnp.zeros_like(hs5)
        hlru[...] = jnp.zeros_like(hlru)

    def row_of(t):
        te = t + d * (TL - 1 - 2 * t)
        return pl.multiple_of(te * BATCH, BATCH)

    prev_ok = jnp.where((c != 0) & (c != NCT), 1.0, 0.0).astype(F32)
    next_ok = jnp.where((c != NCT - 1) & (c != NCH - 1), 1.0, 0.0).astype(F32)
    xpad[0:2 * BATCH, :] = _time_major(xp_ref[...].astype(F32))[(HALO - 2) * BATCH:, :] * prev_ok
    xpad[pl.ds(2 * BATCH, rows), :] = _time_major(xl_ref[...].astype(F32))
    xpad[pl.ds((TL + 2) * BATCH, BATCH), :] = _time_major(xn_ref[...].astype(F32))[0:BATCH, :] * next_ok

    def gates(r0):
        rs = slice(r0, r0 + ROW_BLK)
        xc = cb_ref[...]
        for k in range(LRU_CONV):
            xc = xc + xpad[r0 + k * BATCH:r0 + k * BATCH + ROW_BLK, :] * cw_ref[k:k + 1, :]
        g = jnp.dot(xc.astype(BF16), wg_ref[...], preferred_element_type=F32) + bg_ref[...]
        rg = _sigmoid(g[:, 0:LRU_WIDTH])
        ig = _sigmoid(g[:, LRU_WIDTH:2 * LRU_WIDTH])
        log_a = (-LRU_C * rg) * sp_ref[...]
        a = jnp.exp(log_a)
        z = jnp.maximum(jnp.tanh(-log_a) * (a * a + 1.0), 0.0)
        mult = jnp.where(z > 0.0, z * lax.rsqrt(z), 0.0)
        la[rs, :] = a
        lb[rs, :] = mult * (ig * xc)

    ub = _time_major(us5_ref[...].astype(F32)).astype(BF16)
    n_tiles = 2 * S5_NSTATE // MXU_TILE
    n_gate = rows // ROW_BLK
    for n in range(n_tiles):
        ns = slice(n * MXU_TILE, (n + 1) * MXU_TILE)
        xs[:, ns] = jnp.dot(ub, bb_ref[:, ns], preferred_element_type=F32)
        if n % (n_tiles // n_gate) == 0:
            gates((n // (n_tiles // n_gate)) * ROW_BLK)

    for cb in range(S5_NSTATE // CBW):
        re = slice(cb * CBW, (cb + 1) * CBW)
        im = slice(S5_NSTATE + cb * CBW, S5_NSTATE + (cb + 1) * CBW)
        ar = jnp.broadcast_to(ab_ref[0:1, re], (BATCH, CBW))
        ai = jnp.broadcast_to(ab_ref[1:2, re], (BATCH, CBW))
        with_lru = cb == 0

        def step(t, carry, re=re, im=im, ar=ar, ai=ai, with_lru=with_lru):
            hr, hi = carry[0], carry[1]
            rs = pl.ds(row_of(t), BATCH)
            nr = ar * hr - ai * hi + xs[rs, re]
            ni = ar * hi + ai * hr + xs[rs, im]
            hb[rs, re] = nr.astype(BF16)
            hb[rs, im] = ni.astype(BF16)
            if not with_lru:
                return nr, ni
            hl = la[rs, :] * carry[2] + lb[rs, :]
            lb[rs, :] = hl
            return nr, ni, hl

        init = (hs5[0, :, re], hs5[1, :, re]) + ((hlru[...],) if with_lru else ())
        fin = lax.fori_loop(0, TL, step, init, unroll=4)
        hs5[0, :, re] = fin[0]
        hs5[1, :, re] = fin[1]
        if with_lru:
            hlru[...] = fin[2]

    y = jnp.dot(hb[...], cc_ref[...], preferred_element_type=F32)
    ysc_ref[:, :, 0:S5_WIDTH] = _batch_major(y).astype(ysc_ref.dtype)
    ysc_ref[:, :, S5_WIDTH:] = _batch_major(lb[...]).astype(ysc_ref.dtype)


def _scan(l, proj, bb, cc, ab, conv_w, conv_b, wg, bg, sp):
    n_halo = LT // HALO
    chunk = lambda cb: (lambda d, i: (0, _chunk_index(d, i), cb))
    prev = lambda d, i: (0, jnp.maximum(_chunk_index(d, i) * (TL // HALO) - 1, 0), PROJ_XLRU)
    nxt = lambda d, i: (0, jnp.minimum((_chunk_index(d, i) + 1) * (TL // HALO), n_halo - 1), PROJ_XLRU)
    per_dir = lambda s: pl.BlockSpec((None, None) + s, lambda d, i: (l, d) + (0,) * len(s))
    out_spec = pl.BlockSpec((None, BATCH, TL, 512), lambda d, i: (d, 0, _chunk_index(d, i), 0))
    rows = TL * BATCH
    return pl.pallas_call(
        _scan_body,
        grid=(2, NCH),
        in_specs=[
            pl.BlockSpec((BATCH, TL, S5_WIDTH), chunk(PROJ_US5)),
            pl.BlockSpec((BATCH, TL, LRU_WIDTH), chunk(PROJ_XLRU)),
            pl.BlockSpec((BATCH, HALO, LRU_WIDTH), prev),
            pl.BlockSpec((BATCH, HALO, LRU_WIDTH), nxt),
            per_dir((S5_WIDTH, 2 * S5_NSTATE)),
            per_dir((2 * S5_NSTATE, S5_WIDTH)),
            per_dir((2, S5_NSTATE)),
            _layer_spec(l, (LRU_CONV, LRU_WIDTH)),
            _layer_spec(l, (1, LRU_WIDTH)),
            per_dir((LRU_WIDTH, 2 * LRU_WIDTH)),
            per_dir((1, 2 * LRU_WIDTH)),
            per_dir((1, LRU_WIDTH)),
        ],
        out_specs=out_spec,
        out_shape=jax.ShapeDtypeStruct((2, BATCH, LT, 512), BF16),
        scratch_shapes=[
            pltpu.VMEM((rows, 2 * S5_NSTATE), F32),
            pltpu.VMEM((rows, 2 * S5_NSTATE), BF16),
            pltpu.VMEM(((TL + 3) * BATCH, LRU_WIDTH), F32),
            pltpu.VMEM((rows, LRU_WIDTH), F32),
            pltpu.VMEM((rows, LRU_WIDTH), F32),
            pltpu.VMEM((2, BATCH, S5_NSTATE), F32),
            pltpu.VMEM((BATCH, LRU_WIDTH), F32),
        ],
        compiler_params=pltpu.CompilerParams(vmem_limit_bytes=_VMEM_LIMIT),
        name="s5_lru_scan",
    )(proj, proj, proj, proj, bb, cc, ab, conv_w, conv_b, wg, bg, sp)


def _rope(v, cos, sin):
    width = v.shape[-1]
    lane = lax.broadcasted_iota(jnp.int32, v.shape, 1)
    first_half = ((lane // 8) % 2) == 0
    swapped = jnp.where(first_half, pltpu.roll(v, width - 8, 1), pltpu.roll(v, 8, 1))
    return v * cos + swapped * sin


def _attn_body(cq_ref, kvr_ref, cosq_ref, sinq_ref, cosk_ref, sink_ref, qn_ref, wuq_ref,
               kvn_ref, wukv_ref, o_ref, kt_s, v_s, *, q_off):
    j = pl.program_id(1)

    @pl.when(j == 0)
    def _prep():
        lane = lax.broadcasted_iota(jnp.int32, (ROW_BLK, QK_PAD), 1)
        ones_hi = jnp.where(lane % HEAD_PAD >= MLA_V, 1.0, 0.0).astype(F32)

        def blk(r, carry):
            rs = pl.ds(pl.multiple_of(r * ROW_BLK, ROW_BLK), ROW_BLK)
            kvr = kvr_ref[0, rs, :].astype(F32)
            kvn = _rms(kvr[:, 0:MLA_KV_RANK], kvn_ref[...])
            kv = jnp.dot(kvn.astype(BF16), wukv_ref[...], preferred_element_type=F32)
            kro = _rope(kvr[:, MLA_KV_RANK:], cosk_ref[rs, :], sink_ref[rs, :])
            for h in range(MLA_HEADS):
                hs = slice(h * HEAD_PAD, (h + 1) * HEAD_PAD)
                kt_s[r, hs, :] = (kv[:, hs] + kro).T.astype(BF16)
            v_s[rs, :] = (kv[:, QK_PAD:] + ones_hi).astype(BF16)
            return carry
        lax.fori_loop(0, N_ROW_BLK, blk, 0)

    qn = _rms(cq_ref[0].astype(F32), qn_ref[...])
    q = jnp.dot(qn.astype(BF16), wuq_ref[...], preferred_element_type=F32)
    cosq = cosq_ref[...]
    sinq = sinq_ref[...]
    lane = lax.broadcasted_iota(jnp.int32, (ROW_BLK, HEAD_PAD), 1)

    def head_slice(h):
        return slice(h * HEAD_PAD, (h + 1) * HEAD_PAD)

    def roped_q(h):
        return _rope(q[:, head_slice(h)], cosq, sinq).astype(BF16)

    def row_max(blocks):
        mx = blocks[0]
        for sb in blocks[1:]:
            mx = jnp.maximum(mx, sb)
        return jnp.max(mx, axis=-1, keepdims=True)

    def attend(n_blk):
        score = lambda qh, h, r: jnp.dot(qh, kt_s[r, head_slice(h), :], preferred_element_type=F32)
        q_cur = roped_q(0)
        s_cur = [score(q_cur, 0, r) for r in range(n_blk)]
        m_cur = row_max(s_cur)
        pv = []
        for h in range(MLA_HEADS):
            more = h + 1 < MLA_HEADS
            if more:
                q_nxt = roped_q(h + 1)
            p_blocks, s_nxt = [], []
            for r in range(n_blk):
                p_blocks.append(jnp.exp2(s_cur[r] - m_cur).astype(BF16))
                if more:
                    s_nxt.append(score(q_nxt, h + 1, r))
            p = jnp.concatenate(p_blocks, axis=-1)
            pv.append(jnp.dot(p, v_s[0:n_blk * ROW_BLK, head_slice(h)], preferred_element_type=F32))
            if more:
                s_cur, m_cur = s_nxt, row_max(s_nxt)
            if h % 2 == 1:
                o_even = pv[h - 1] / pltpu.roll(pv[h - 1], MLA_V, 1)
                o_odd = pltpu.roll(pv[h], MLA_V, 1) / pv[h]
                o_ref[0, :, (h // 2) * LANES:(h // 2 + 1) * LANES] = jnp.where(
                    lane < MLA_V, o_even, o_odd).astype(o_ref.dtype)

    is_ctx = (j + q_off) == 0

    @pl.when(is_ctx)
    def _():
        attend(CTX_LEN // ROW_BLK)

    @pl.when(jnp.logical_not(is_ctx))
    def _():
        attend(N_ROW_BLK)


def _attention(l, proj, cosq, sinq, cosk, sink, q_norm, w_uq_pad, kv_norm, w_ukv_pad, with_ctx):
    q_off = 0 if with_ctx else 1
    const = lambda s: pl.BlockSpec(s, lambda b, j: (0,) * len(s))
    return pl.pallas_call(
        functools.partial(_attn_body, q_off=q_off),
        grid=(BATCH, N_ROW_BLK - q_off),
        in_specs=[
            pl.BlockSpec((1, ROW_BLK, MLA_Q_RANK), lambda b, j: (b, j + q_off, PROJ_CQ)),
            pl.BlockSpec((1, LT, 256), lambda b, j: (b, 0, PROJ_KVR)),
            pl.BlockSpec((ROW_BLK, HEAD_PAD), lambda b, j: (j + q_off, 0)),
            pl.BlockSpec((ROW_BLK, HEAD_PAD), lambda b, j: (j + q_off, 0)),
            const((LT, HEAD_PAD)),
            const((LT, HEAD_PAD)),
            _layer_spec(l, (1, MLA_Q_RANK)),
            _layer_spec(l, (MLA_Q_RANK, QK_PAD)),
            _layer_spec(l, (1, MLA_KV_RANK)),
            _layer_spec(l, (MLA_KV_RANK, 2 * QK_PAD)),
        ],
        out_specs=pl.BlockSpec((1, ROW_BLK, MLA_WIDTH), lambda b, j: (b, j + q_off, 0)),
        out_shape=jax.ShapeDtypeStruct((BATCH, LT, MLA_WIDTH), BF16),
        scratch_shapes=[
            pltpu.VMEM((N_ROW_BLK, QK_PAD, ROW_BLK), BF16),
            pltpu.VMEM((LT, QK_PAD), BF16),
        ],
        compiler_params=pltpu.CompilerParams(vmem_limit_bytes=_VMEM_LIMIT),
        name="mla_attention",
    )(proj, proj, cosq, sinq, cosk, sink, q_norm, w_uq_pad, kv_norm, w_ukv_pad)


def _block_diag(blocks):
    n, r, c = blocks.shape
    eye = jnp.eye(n, dtype=blocks.dtype)
    return (eye[:, None, :, None] * blocks[:, :, None, :]).reshape(n * r, n * c)


def _s5_matrices(lam_re, lam_im, log_dt, b_re, b_im, c_re, c_im):
    dt = jnp.exp(log_dt)[:, None]
    mag = jnp.exp(lam_re * dt)
    ab_re, ab_im = mag * jnp.cos(lam_im * dt), mag * jnp.sin(lam_im * dt)
    den = lam_re * lam_re + lam_im * lam_im
    nr = ab_re - 1.0
    coef_re = (nr * lam_re + ab_im * lam_im) / den
    coef_im = (ab_im * lam_re - nr * lam_im) / den
    bb_re = coef_re[..., None] * b_re - coef_im[..., None] * b_im
    bb_im = coef_re[..., None] * b_im + coef_im[..., None] * b_re
    to_in = lambda m: _block_diag(jnp.swapaxes(m, 1, 2))
    bb = jnp.concatenate([to_in(bb_re), to_in(bb_im)], axis=1)
    to_out = lambda m: _block_diag(jnp.swapaxes(m, 1, 2))
    cc = jnp.concatenate([to_out(c_re), -to_out(c_im)], axis=0)
    ab = jnp.stack([ab_re.reshape(-1), ab_im.reshape(-1)])
    return bb.astype(BF16), cc.astype(BF16), ab


def _rope_tables():
    n_rows = SEQ // GRID_W
    rows = jnp.repeat(jnp.arange(n_rows, dtype=F32), GRID_W)
    cols = jnp.tile(jnp.arange(GRID_W, dtype=F32), n_rows)
    inv_freq = ROPE_BASE ** (-jnp.arange(0, ROPE_AXIS, 2, dtype=F32) / ROPE_AXIS)
    ang_r = rows[:, None] * inv_freq
    ang_c = cols[:, None] * inv_freq
    cr, sr, cc, sc = jnp.cos(ang_r), jnp.sin(ang_r), jnp.cos(ang_c), jnp.sin(ang_c)
    cos32 = jnp.concatenate([cr, cr, cc, cc], axis=-1)
    sin32 = jnp.concatenate([-sr, sr, -sc, sc], axis=-1)
    cos = jnp.ones((LT, HEAD_PAD), F32).at[CTX_LEN:, MLA_NOPE:MLA_QK].set(cos32)
    sin = jnp.zeros((LT, HEAD_PAD), F32).at[CTX_LEN:, MLA_NOPE:MLA_QK].set(sin32)
    return cos, sin


def _pad_heads(w, used):
    w = jnp.pad(w, ((0, 0), (0, 0), (0, 0), (0, HEAD_PAD - used)))
    return w.reshape(w.shape[0], w.shape[1], MLA_HEADS * HEAD_PAD)


def kernel(x, c, ctx, c_ctx, ada_w, ada_b, norm_g, w_in, s5_lam_re, s5_lam_im, s5_log_dt, s5_b_re,
           s5_b_im, s5_c_re, s5_c_im, s5_d, s5_glu_w, s5_glu_b, lru_conv_w, lru_conv_b, lru_wa,
           lru_ba, lru_wx, lru_bx, lru_lam, mla_q_norm, mla_w_uq, mla_kv_norm, mla_w_ukv, w_out,
           final_g):
    c_pad = jnp.zeros((MOD_ROWS, D_MODEL), F32).at[:BATCH].set(c).at[BATCH].set(c_ctx)
    mod = _mod_tables(c_pad, ada_w, ada_b)

    cosk, sink = _rope_tables()
    q_mul = ATTN_SCALE * math.log2(math.e)
    cosq, sinq = cosk * q_mul, sink * q_mul

    kr_pad = jnp.zeros((DEPTH, D_MODEL, HEAD_PAD), F32).at[:, :, MLA_NOPE:MLA_QK].set(w_in[:, :, 1408:1440])
    w_in_pad = jnp.concatenate([w_in[:, :, 0:1408], kr_pad, w_in[:, :, 1440:1952]], axis=2).astype(BF16)
    bb, cc, ab = jax.vmap(jax.vmap(_s5_matrices))(s5_lam_re, s5_lam_im, s5_log_dt, s5_b_re, s5_b_im,
                                                  s5_c_re, s5_c_im)
    bd = jax.vmap(jax.vmap(_block_diag))
    wg = jnp.concatenate([bd(lru_wa), bd(lru_wx)], axis=-1).astype(BF16)
    bg = jnp.concatenate([lru_ba, lru_bx], axis=-1)[:, :, None, :]
    sp = jax.nn.softplus(-lru_lam)[:, :, None, :]
    w_uq_pad = _pad_heads(mla_w_uq.reshape(DEPTH, MLA_Q_RANK, MLA_HEADS, MLA_QK), MLA_QK).astype(BF16)
    ukv = mla_w_ukv.reshape(DEPTH, MLA_KV_RANK, MLA_HEADS, MLA_NOPE + MLA_V)
    w_ukv_pad = jnp.concatenate([_pad_heads(ukv[..., :MLA_NOPE], MLA_NOPE),
                                 _pad_heads(ukv[..., MLA_NOPE:], MLA_V)], axis=-1).astype(BF16)
    glu_w = s5_glu_w.astype(BF16)
    w_out_b = w_out.astype(BF16)
    row = lambda a: a[:, None, :]

    proj_w = (row(norm_g), w_in_pad)
    x_src = (ctx, x)
    (proj,) = _row_stage(0, x_src, mod, None, proj_w, None, merge=False, project=True)
    for l in range(DEPTH):
        final = l == DEPTH - 1
        yscan = _scan(l, proj, bb, cc, ab, lru_conv_w, row(lru_conv_b), wg, bg, sp)
        ymla = _attention(l, proj, cosq, sinq, cosk, sink, row(mla_q_norm), w_uq_pad,
                          row(mla_kv_norm), w_ukv_pad, with_ctx=not final)
        merge_in = (yscan, proj, ymla, row(s5_d), glu_w, row(s5_glu_b), w_out_b)
        if final:
            (out,) = _row_stage(l, x_src, mod, merge_in, None, final_g[None], merge=True,
                                project=False, final=True)
            return out
        x_all, proj = _row_stage(l, x_src, mod, merge_in, proj_w, None, merge=True, project=True)
        x_src = (x_all,)
```

```python
import functools
import math

import jax
import jax.numpy as jnp
from jax import lax
from jax.experimental import pallas as pl
from jax.experimental.pallas import tpu as pltpu

F32 = jnp.float32
BF16 = jnp.bfloat16

D_MODEL = 1024
BATCH = 16
SEQ = 2048
DEPTH = 4
GRID_W = 64
CTX_LEN = 256
LT = CTX_LEN + SEQ
S5_WIDTH = 256
S5_GROUPS = 16
S5_STATE = 64
S5_NSTATE = S5_GROUPS * S5_STATE
LRU_WIDTH = 256
LRU_CONV = 4
LRU_C = 8.0
MLA_HEADS = 8
MLA_NOPE = 64
MLA_QK = 96
MLA_V = 64
MLA_WIDTH = 512
MLA_Q_RANK = 256
MLA_KV_RANK = 128
ATTN_SCALE = MLA_QK ** -0.5
ROPE_AXIS = 16
ROPE_BASE = 10000.0
EPS = 1e-6

LANES = 128
HEAD_PAD = LANES
QK_PAD = MLA_HEADS * HEAD_PAD
IN_PAD = 2048
MOD_ROWS = 24
ROW_BLK = 256
N_ROW_BLK = LT // ROW_BLK
TL = 64
NCT = CTX_LEN // TL
NLT = SEQ // TL
NCH = NCT + NLT
HALO = 16
CBW = 512
MXU_TILE = 256

_VMEM_LIMIT = 48 * 1024 * 1024


def _split_bf16(v):
    hi = v.astype(BF16)
    lo = (v - hi.astype(F32)).astype(BF16)
    return hi, lo


def _sigmoid(v):
    return 0.5 * jnp.tanh(0.5 * v) + 0.5


def _silu(v):
    return v * _sigmoid(v)


def _rms(v, g):
    ms = jnp.mean(v * v, axis=-1, keepdims=True)
    return v * lax.rsqrt(ms + EPS) * g


def _layer_spec(l, shape):
    zeros = (0,) * len(shape)
    return pl.BlockSpec((None,) + shape, lambda a, b: (l,) + zeros)


def _mod_body(c_ref, w_ref, b_ref, o_ref):
    c = c_ref[...]
    sc = c * jax.nn.sigmoid(c)
    s_hi, s_lo = _split_bf16(sc)
    w_hi, w_lo = _split_bf16(w_ref[0])
    acc = jnp.dot(s_hi, w_hi, preferred_element_type=F32)
    acc += jnp.dot(s_hi, w_lo, preferred_element_type=F32)
    acc += jnp.dot(s_lo, w_hi, preferred_element_type=F32)
    o_ref[0] = acc + b_ref[0]


def _mod_tables(c_pad, ada_w, ada_b):
    nj = 3 * D_MODEL // 1024
    return pl.pallas_call(
        _mod_body,
        grid=(DEPTH, nj),
        in_specs=[
            pl.BlockSpec((MOD_ROWS, D_MODEL), lambda l, j: (0, 0)),
            pl.BlockSpec((1, D_MODEL, 1024), lambda l, j: (l, 0, j)),
            pl.BlockSpec((1, 1, 1024), lambda l, j: (l, 0, j)),
        ],
        out_specs=pl.BlockSpec((1, MOD_ROWS, 1024), lambda l, j: (l, 0, j)),
        out_shape=jax.ShapeDtypeStruct((DEPTH, MOD_ROWS, 3 * D_MODEL), F32),
        compiler_params=pltpu.CompilerParams(vmem_limit_bytes=_VMEM_LIMIT),
        name="adaln_tables",
    )(c_pad, ada_w, ada_b.reshape(DEPTH, 1, 3 * D_MODEL))


PROJ_US5, PROJ_XLRU, PROJ_ZLRU, PROJ_CQ, PROJ_KVR = 0, 2, 3, 4, 5


def _row_body(*refs, layer, merge, project, final, split_src):
    it = iter(refs)
    ctx_ref = next(it) if split_src else None
    x_ref, mod_ref = next(it), next(it)
    if merge:
        yf_ref, yb_ref, uz_ref, zlru_ref, ymla_ref, zmla_ref, d_ref, gw_ref, gb_ref, wo_ref = (
            next(it) for _ in range(10))
    fg_ref = next(it) if final else None
    if project:
        g_ref, wi_ref = next(it), next(it)
    outs = list(it)

    b = pl.program_id(0)
    t = pl.program_id(1) + (1 if final else 0)
    is_ctx = t == 0
    mod_row = jnp.where(is_ctx, BATCH, b)
    x = x_ref[0]
    if split_src:
        x = jnp.where(is_ctx, ctx_ref[0], x)

    if merge:
        gate = mod_ref[layer, pl.ds(mod_row, 1), 2 * D_MODEL:3 * D_MODEL]
        ysc = yf_ref[0].astype(F32) + yb_ref[0].astype(F32)
        uz = uz_ref[0].astype(F32)
        y = d_ref[...] * uz[:, 0:S5_WIDTH] + ysc[:, 0:S5_WIDTH]
        g = jnp.dot(y.astype(BF16), gw_ref[...], preferred_element_type=F32) + gb_ref[...]
        y_s5 = g[:, 0:S5_WIDTH] * _sigmoid(g[:, S5_WIDTH:])
        cat = jnp.concatenate([
            (y_s5 * _silu(uz[:, S5_WIDTH:])).astype(BF16),
            (ysc[:, S5_WIDTH:] * _silu(zlru_ref[0].astype(F32))).astype(BF16),
            (ymla_ref[0].astype(F32) * _silu(zmla_ref[0].astype(F32))).astype(BF16)], axis=-1)
        x = x + gate * jnp.dot(cat, wo_ref[...], preferred_element_type=F32)
        if final:
            outs.pop(0)[0] = _rms(x, fg_ref[...])
        else:
            outs.pop(0)[0] = x

    if project:
        nxt = layer + 1 if merge else layer
        shift = mod_ref[nxt, pl.ds(mod_row, 1), 0:D_MODEL]
        scale = mod_ref[nxt, pl.ds(mod_row, 1), D_MODEL:2 * D_MODEL]
        h = _rms(x, g_ref[...]) * (1.0 + scale) + shift
        outs.pop(0)[0] = jnp.dot(h.astype(BF16), wi_ref[...],
                                 preferred_element_type=F32).astype(BF16)


def _row_stage(layer, x_src, mod, merge_in, proj_w, final_g, *, merge, project, final=False):
    split_src = len(x_src) == 2
    t_off = 1 if final else 0
    row_spec = lambda w, cb=0: pl.BlockSpec((1, ROW_BLK, w), lambda b, t: (b, t + t_off, cb))
    const = lambda s: pl.BlockSpec(s, lambda b, t: (0,) * len(s))
    args, in_specs = [], []
    if split_src:
        args += list(x_src)
        in_specs += [pl.BlockSpec((1, ROW_BLK, D_MODEL), lambda b, t: (b, 0, 0)),
                     pl.BlockSpec((1, ROW_BLK, D_MODEL), lambda b, t: (b, jnp.maximum(t - 1, 0), 0))]
    else:
        args += list(x_src)
        in_specs += [row_spec(D_MODEL)]
    args.append(mod)
    in_specs.append(const((DEPTH, MOD_ROWS, 3 * D_MODEL)))
    if merge:
        y_fwd, y_bwd, proj, ymla, s5_d, glu_w, glu_b, w_out = merge_in
        args += [y_fwd, y_bwd, proj, proj, ymla, proj, s5_d, glu_w, glu_b, w_out]
        in_specs += [
            row_spec(512), row_spec(512),
            row_spec(512, 0),
            row_spec(256, PROJ_ZLRU),
            row_spec(MLA_WIDTH),
            row_spec(512, 3),
            _layer_spec(layer, (1, S5_WIDTH)),
            _layer_spec(layer, (S5_WIDTH, 2 * S5_WIDTH)),
            _layer_spec(layer, (1, 2 * S5_WIDTH)),
            _layer_spec(layer, (D_MODEL, D_MODEL)),
        ]
    if final:
        args.append(final_g)
        in_specs.append(const((1, D_MODEL)))
    if project:
        nxt = layer + 1 if merge else layer
        norm_g, w_in_pad = proj_w
        args += [norm_g, w_in_pad]
        in_specs += [_layer_spec(nxt, (1, D_MODEL)), _layer_spec(nxt, (D_MODEL, IN_PAD))]

    out_shape, out_specs, aliases = [], [], {}
    if merge:
        if final:
            out_shape.append(jax.ShapeDtypeStruct((BATCH, SEQ, D_MODEL), F32))
            out_specs.append(pl.BlockSpec((1, ROW_BLK, D_MODEL), lambda b, t: (b, t, 0)))
        else:
            out_shape.append(jax.ShapeDtypeStruct((BATCH, LT, D_MODEL), F32))
            out_specs.append(row_spec(D_MODEL))
            if not split_src:
                aliases = {0: 0}
    if project:
        out_shape.append(jax.ShapeDtypeStruct((BATCH, LT, IN_PAD), BF16))
        out_specs.append(row_spec(IN_PAD))
    return pl.pallas_call(
        functools.partial(_row_body, layer=layer, merge=merge, project=project, final=final,
                          split_src=split_src),
        grid=(BATCH, N_ROW_BLK - t_off),
        in_specs=in_specs,
        out_specs=out_specs,
        out_shape=out_shape,
        input_output_aliases=aliases,
        compiler_params=pltpu.CompilerParams(vmem_limit_bytes=_VMEM_LIMIT),
        name="row_stage",
    )(*args)


def _chunk_index(bwd, i):
    if not bwd:
        return i
    return jnp.where(i < NCT, NCT - 1 - i, 2 * NCT + NLT - 1 - i)


def _time_major(v):
    t = v.shape[1]
    return jnp.swapaxes(v, 0, 1).reshape(t * BATCH, v.shape[2])


def _batch_major(v):
    return jnp.swapaxes(v.reshape(v.shape[0] // BATCH, BATCH, v.shape[1]), 0, 1)


def _scan_body(us5_ref, xl_ref, xp_ref, xn_ref, bb_ref, cc_ref, ab_ref, cw_ref, cb_ref,
               wg_ref, bg_ref, sp_ref, ysc_ref,
               xs, hb, xpad, la, lb, hs5, hlru, *, bwd):
    i = pl.program_id(0)
    c = _chunk_index(bwd, i)
    rows = TL * BATCH
    half = rows // 2

    @pl.when(i == 0)
    def _():
        hs5[...] = jnp.zeros_like(hs5)
        hlru[...] = jnp.zeros_like(hlru)

    def rows_of(k):
        t = TL - 1 - k if bwd else k
        return slice(t * BATCH, (t + 1) * BATCH)

    def half_rows(k):
        first = (1 - k) if bwd else k
        return slice(first * half, (first + 1) * half)

    prev_ok = jnp.where((c != 0) & (c != NCT), 1.0, 0.0).astype(F32)
    next_ok = jnp.where((c != NCT - 1) & (c != NCH - 1), 1.0, 0.0).astype(F32)
    xpad[0:2 * BATCH, :] = _time_major(xp_ref[...])[(HALO - 2) * BATCH:, :].astype(F32) * prev_ok
    xpad[pl.ds(2 * BATCH, rows), :] = _time_major(xl_ref[...]).astype(F32)
    xpad[pl.ds((TL + 2) * BATCH, BATCH), :] = _time_major(xn_ref[...])[0:BATCH, :].astype(F32) * next_ok

    def gates(r0):
        rs = slice(r0, r0 + ROW_BLK)
        xc = cb_ref[...]
        for k in range(LRU_CONV):
            xc = xc + xpad[r0 + k * BATCH:r0 + k * BATCH + ROW_BLK, :] * cw_ref[k:k + 1, :]
        th = jnp.tanh(jnp.dot(xc.astype(BF16), wg_ref[...], preferred_element_type=F32) + bg_ref[...])
        sp4 = sp_ref[...]
        neg_log_a = sp4 * th[:, 0:LRU_WIDTH] + sp4
        a = jnp.exp2(neg_log_a * (-math.log2(math.e)))
        z = jnp.maximum(jnp.tanh(neg_log_a) * (a * a + 1.0), 0.0)
        mult = jnp.where(z > 0.0, z * lax.rsqrt(z), 0.0)
        la[rs, :] = a
        lb[rs, :] = (mult * xc) * (0.5 * th[:, LRU_WIDTH:] + 0.5)

    ub = _time_major(us5_ref[...])
    blk_w = 2 * CBW
    n_blk = S5_NSTATE // CBW
    n_tiles = 2 * S5_NSTATE // MXU_TILE
    half_steps = TL // 2

    def b_tile(k, n):
        hr, ns = half_rows(k), slice(n * MXU_TILE, (n + 1) * MXU_TILE)
        xs[hr, ns] = jnp.dot(ub[hr], bb_ref[:, ns], preferred_element_type=F32)

    def load_carry(cb):
        st = slice(cb * CBW, (cb + 1) * CBW)
        return (hs5[0, :, st], hs5[1, :, st]) + ((hlru[...],) if cb == 0 else ())

    def save_carry(cb, carry):
        st = slice(cb * CBW, (cb + 1) * CBW)
        hs5[0, :, st] = carry[0]
        hs5[1, :, st] = carry[1]
        if cb == 0:
            hlru[...] = carry[2]

    def scan_steps(cb, carry, k0, k1):
        st = slice(cb * CBW, (cb + 1) * CBW)
        re = slice(cb * blk_w, cb * blk_w + CBW)
        im = slice(cb * blk_w + CBW, (cb + 1) * blk_w)
        ar = jnp.broadcast_to(ab_ref[0:1, st], (BATCH, CBW))
        ai = jnp.broadcast_to(ab_ref[1:2, st], (BATCH, CBW))
        for k in range(k0, k1):
            rs = rows_of(k)
            hr, hi = carry[0], carry[1]
            nr = ar * hr - ai * hi + xs[rs, re]
            ni = ar * hi + ai * hr + xs[rs, im]
            hb[rs, re] = nr.astype(BF16)
            hb[rs, im] = ni.astype(BF16)
            if cb == 0:
                hl = la[rs, :] * carry[2] + lb[rs, :]
                lb[rs, :] = hl
                carry = (nr, ni, hl)
            else:
                carry = (nr, ni)
        return carry

    def c_map(k):
        return jnp.dot(hb[half_rows(k), :], cc_ref[...], preferred_element_type=F32)

    gate_blocks = list(range(rows // ROW_BLK))
    if bwd:
        gate_blocks.reverse()
    for n in range(n_tiles):
        b_tile(0, n)
        if n % 2 == 0:
            gates(gate_blocks[n // 2] * ROW_BLK)
    tiles_per_blk = n_tiles // n_blk
    seg = half_steps // tiles_per_blk
    for cb in range(n_blk):
        carry = load_carry(cb)
        for n in range(tiles_per_blk):
            carry = scan_steps(cb, carry, n * seg, (n + 1) * seg)
            b_tile(1, cb * tiles_per_blk + n)
        save_carry(cb, carry)
    y_a = c_map(0)
    for cb in range(n_blk):
        save_carry(cb, scan_steps(cb, load_carry(cb), half_steps, TL))
    y_b = c_map(1)
    y = jnp.concatenate([y_b, y_a] if bwd else [y_a, y_b], axis=0)
    ysc_ref[:, :, 0:S5_WIDTH] = _batch_major(y.astype(ysc_ref.dtype))
    ysc_ref[:, :, S5_WIDTH:] = _batch_major(lb[...].astype(ysc_ref.dtype))


def _scan(l, d, proj, bb, cc, ab, conv_w, conv_b, wg, bg, sp):
    bwd = d == 1
    n_halo = LT // HALO
    chunk = lambda cb: (lambda i: (0, _chunk_index(bwd, i), cb))
    prev = lambda i: (0, jnp.maximum(_chunk_index(bwd, i) * (TL // HALO) - 1, 0), PROJ_XLRU)
    nxt = lambda i: (0, jnp.minimum((_chunk_index(bwd, i) + 1) * (TL // HALO), n_halo - 1), PROJ_XLRU)
    per_dir = lambda s: pl.BlockSpec((None, None) + s, lambda i: (l, d) + (0,) * len(s))
    per_layer = lambda s: pl.BlockSpec((None,) + s, lambda i: (l,) + (0,) * len(s))
    out_spec = pl.BlockSpec((BATCH, TL, 512), lambda i: (0, _chunk_index(bwd, i), 0))
    rows = TL * BATCH
    return pl.pallas_call(
        functools.partial(_scan_body, bwd=bwd),
        grid=(NCH,),
        in_specs=[
            pl.BlockSpec((BATCH, TL, S5_WIDTH), chunk(PROJ_US5)),
            pl.BlockSpec((BATCH, TL, LRU_WIDTH), chunk(PROJ_XLRU)),
            pl.BlockSpec((BATCH, HALO, LRU_WIDTH), prev),
            pl.BlockSpec((BATCH, HALO, LRU_WIDTH), nxt),
            per_dir((S5_WIDTH, 2 * S5_NSTATE)),
            per_dir((2 * S5_NSTATE, S5_WIDTH)),
            per_dir((2, S5_NSTATE)),
            per_layer((LRU_CONV, LRU_WIDTH)),
            per_layer((1, LRU_WIDTH)),
            per_dir((LRU_WIDTH, 2 * LRU_WIDTH)),
            per_dir((1, 2 * LRU_WIDTH)),
            per_dir((1, LRU_WIDTH)),
        ],
        out_specs=out_spec,
        out_shape=jax.ShapeDtypeStruct((BATCH, LT, 512), BF16),
        scratch_shapes=[
            pltpu.VMEM((rows, 2 * S5_NSTATE), F32),
            pltpu.VMEM((rows, 2 * S5_NSTATE), BF16),
            pltpu.VMEM(((TL + 3) * BATCH, LRU_WIDTH), F32),
            pltpu.VMEM((rows, LRU_WIDTH), F32),
            pltpu.VMEM((rows, LRU_WIDTH), F32),
            pltpu.VMEM((2, BATCH, S5_NSTATE), F32),
            pltpu.VMEM((BATCH, LRU_WIDTH), F32),
        ],
        compiler_params=pltpu.CompilerParams(vmem_limit_bytes=_VMEM_LIMIT),
        name="s5_lru_scan",
    )(proj, proj, proj, proj, bb, cc, ab, conv_w, conv_b, wg, bg, sp)


def _rope(v, cos, sin):
    width = v.shape[-1]
    lane = lax.broadcasted_iota(jnp.int32, v.shape, 1)
    first_half = ((lane // 8) % 2) == 0
    swapped = jnp.where(first_half, pltpu.roll(v, width - 8, 1), pltpu.roll(v, 8, 1))
    return v * cos + swapped * sin


def _attn_body(cq_ref, kvr_ref, cosq_ref, sinq_ref, cosk_ref, sink_ref, qn_ref, wuq_ref,
               kvn_ref, wukv_ref, o_ref, kt_s, v_s, *, q_off):
    j = pl.program_id(1)

    @pl.when(j == 0)
    def _prep():
        lane = lax.broadcasted_iota(jnp.int32, (ROW_BLK, QK_PAD), 1)
        ones_hi = jnp.where(lane % HEAD_PAD >= MLA_V, 1.0, 0.0).astype(F32)

        def blk(r, carry):
            rs = pl.ds(pl.multiple_of(r * ROW_BLK, ROW_BLK), ROW_BLK)
            kvr = kvr_ref[0, rs, :].astype(F32)
            kvn = _rms(kvr[:, 0:MLA_KV_RANK], kvn_ref[...])
            kv = jnp.dot(kvn.astype(BF16), wukv_ref[...], preferred_element_type=F32)
            kro = _rope(kvr[:, MLA_KV_RANK:], cosk_ref[rs, :], sink_ref[rs, :])
            for h in range(MLA_HEADS):
                hs = slice(h * HEAD_PAD, (h + 1) * HEAD_PAD)
                kt_s[r, hs, :] = (kv[:, hs] + kro).T.astype(BF16)
            v_s[rs, :] = (kv[:, QK_PAD:] + ones_hi).astype(BF16)
            return carry
        lax.fori_loop(0, N_ROW_BLK, blk, 0)

    qn = _rms(cq_ref[0].astype(F32), qn_ref[...])
    q = jnp.dot(qn.astype(BF16), wuq_ref[...], preferred_element_type=F32)
    cosq = cosq_ref[...]
    sinq = sinq_ref[...]
    lane = lax.broadcasted_iota(jnp.int32, (ROW_BLK, HEAD_PAD), 1)

    def head_slice(h):
        return slice(h * HEAD_PAD, (h + 1) * HEAD_PAD)

    def roped_q(h):
        return _rope(q[:, head_slice(h)], cosq, sinq).astype(BF16)

    def row_max(blocks):
        mx = blocks[0]
        for sb in blocks[1:]:
            mx = jnp.maximum(mx, sb)
        return jnp.max(mx, axis=-1, keepdims=True)

    def attend(n_blk):
        score = lambda qh, h, r: jnp.dot(qh, kt_s[r, head_slice(h), :], preferred_element_type=F32)
        q_cur = roped_q(0)
        s_cur = [score(q_cur, 0, r) for r in range(n_blk)]
        m_cur = row_max(s_cur)
        pv = []
        for h in range(MLA_HEADS):
            more = h + 1 < MLA_HEADS
            if more:
                q_nxt = roped_q(h + 1)
            p_blocks, s_nxt = [], []
            for r in range(n_blk):
                p_blocks.append(jnp.exp2(s_cur[r] - m_cur).astype(BF16))
                if more:
                    s_nxt.append(score(q_nxt, h + 1, r))
            p = jnp.concatenate(p_blocks, axis=-1)
            pv.append(jnp.dot(p, v_s[0:n_blk * ROW_BLK, head_slice(h)], preferred_element_type=F32))
            if more:
                s_cur, m_cur = s_nxt, row_max(s_nxt)
            if h % 2 == 1:
                o_even = pv[h - 1] / pltpu.roll(pv[h - 1], MLA_V, 1)
                o_odd = pltpu.roll(pv[h], MLA_V, 1) / pv[h]
                o_ref[0, :, (h // 2) * LANES:(h // 2 + 1) * LANES] = jnp.where(
                    lane < MLA_V, o_even, o_odd).astype(o_ref.dtype)

    is_ctx = (j + q_off) == 0

    @pl.when(is_ctx)
    def _():
        attend(CTX_LEN // ROW_BLK)

    @pl.when(jnp.logical_not(is_ctx))
    def _():
        attend(N_ROW_BLK)


def _attention(l, proj, cosq, sinq, cosk, sink, q_norm, w_uq_pad, kv_norm, w_ukv_pad, with_ctx):
    q_off = 0 if with_ctx else 1
    const = lambda s: pl.BlockSpec(s, lambda b, j: (0,) * len(s))
    return pl.pallas_call(
        functools.partial(_attn_body, q_off=q_off),
        grid=(BATCH, N_ROW_BLK - q_off),
        in_specs=[
            pl.BlockSpec((1, ROW_BLK, MLA_Q_RANK), lambda b, j: (b, j + q_off, PROJ_CQ)),
            pl.BlockSpec((1, LT, 256), lambda b, j: (b, 0, PROJ_KVR)),
            pl.BlockSpec((ROW_BLK, HEAD_PAD), lambda b, j: (j + q_off, 0)),
            pl.BlockSpec((ROW_BLK, HEAD_PAD), lambda b, j: (j + q_off, 0)),
            const((LT, HEAD_PAD)),
            const((LT, HEAD_PAD)),
            _layer_spec(l, (1, MLA_Q_RANK)),
            _layer_spec(l, (MLA_Q_RANK, QK_PAD)),
            _layer_spec(l, (1, MLA_KV_RANK)),
            _layer_spec(l, (MLA_KV_RANK, 2 * QK_PAD)),
        ],
        out_specs=pl.BlockSpec((1, ROW_BLK, MLA_WIDTH), lambda b, j: (b, j + q_off, 0)),
        out_shape=jax.ShapeDtypeStruct((BATCH, LT, MLA_WIDTH), BF16),
        scratch_shapes=[
            pltpu.VMEM((N_ROW_BLK, QK_PAD, ROW_BLK), BF16),
            pltpu.VMEM((LT, QK_PAD), BF16),
        ],
        compiler_params=pltpu.CompilerParams(vmem_limit_bytes=_VMEM_LIMIT),
        name="mla_attention",
    )(proj, proj, cosq, sinq, cosk, sink, q_norm, w_uq_pad, kv_norm, w_ukv_pad)


def _block_diag(blocks):
    n, r, c = blocks.shape
    eye = jnp.eye(n, dtype=blocks.dtype)
    return (eye[:, None, :, None] * blocks[:, :, None, :]).reshape(n * r, n * c)


def _s5_matrices(lam_re, lam_im, log_dt, b_re, b_im, c_re, c_im):
    dt = jnp.exp(log_dt)[:, None]
    mag = jnp.exp(lam_re * dt)
    ab_re, ab_im = mag * jnp.cos(lam_im * dt), mag * jnp.sin(lam_im * dt)
    den = lam_re * lam_re + lam_im * lam_im
    nr = ab_re - 1.0
    coef_re = (nr * lam_re + ab_im * lam_im) / den
    coef_im = (ab_im * lam_re - nr * lam_im) / den
    bb_re = coef_re[..., None] * b_re - coef_im[..., None] * b_im
    bb_im = coef_re[..., None] * b_im + coef_im[..., None] * b_re
    to_in = lambda m: _block_diag(jnp.swapaxes(m, 1, 2))
    to_out = lambda m: _block_diag(jnp.swapaxes(m, 1, 2))

    def grouped(re, im, axis):
        parts = []
        for cb in range(S5_NSTATE // CBW):
            parts += [lax.slice_in_dim(re, cb * CBW, (cb + 1) * CBW, axis=axis),
                      lax.slice_in_dim(im, cb * CBW, (cb + 1) * CBW, axis=axis)]
        return jnp.concatenate(parts, axis=axis)

    bb = grouped(to_in(bb_re), to_in(bb_im), 1)
    cc = grouped(to_out(c_re), -to_out(c_im), 0)
    ab = jnp.stack([ab_re.reshape(-1), ab_im.reshape(-1)])
    return bb.astype(BF16), cc.astype(BF16), ab


def _rope_tables():
    n_rows = SEQ // GRID_W
    rows = jnp.repeat(jnp.arange(n_rows, dtype=F32), GRID_W)
    cols = jnp.tile(jnp.arange(GRID_W, dtype=F32), n_rows)
    inv_freq = ROPE_BASE ** (-jnp.arange(0, ROPE_AXIS, 2, dtype=F32) / ROPE_AXIS)
    ang_r = rows[:, None] * inv_freq
    ang_c = cols[:, None] * inv_freq
    cr, sr, cc, sc = jnp.cos(ang_r), jnp.sin(ang_r), jnp.cos(ang_c), jnp.sin(ang_c)
    cos32 = jnp.concatenate([cr, cr, cc, cc], axis=-1)
    sin32 = jnp.concatenate([-sr, sr, -sc, sc], axis=-1)
    cos = jnp.ones((LT, HEAD_PAD), F32).at[CTX_LEN:, MLA_NOPE:MLA_QK].set(cos32)
    sin = jnp.zeros((LT, HEAD_PAD), F32).at[CTX_LEN:, MLA_NOPE:MLA_QK].set(sin32)
    return cos, sin


def _pad_heads(w, used):
    w = jnp.pad(w, ((0, 0), (0, 0), (0, 0), (0, HEAD_PAD - used)))
    return w.reshape(w.shape[0], w.shape[1], MLA_HEADS * HEAD_PAD)


def kernel(x, c, ctx, c_ctx, ada_w, ada_b, norm_g, w_in, s5_lam_re, s5_lam_im, s5_log_dt, s5_b_re,
           s5_b_im, s5_c_re, s5_c_im, s5_d, s5_glu_w, s5_glu_b, lru_conv_w, lru_conv_b, lru_wa,
           lru_ba, lru_wx, lru_bx, lru_lam, mla_q_norm, mla_w_uq, mla_kv_norm, mla_w_ukv, w_out,
           final_g):
    c_pad = jnp.zeros((MOD_ROWS, D_MODEL), F32).at[:BATCH].set(c).at[BATCH].set(c_ctx)
    mod = _mod_tables(c_pad, ada_w, ada_b)

    cosk, sink = _rope_tables()
    q_mul = ATTN_SCALE * math.log2(math.e)
    cosq, sinq = cosk * q_mul, sink * q_mul

    kr_pad = jnp.zeros((DEPTH, D_MODEL, HEAD_PAD), F32).at[:, :, MLA_NOPE:MLA_QK].set(w_in[:, :, 1408:1440])
    w_in_pad = jnp.concatenate([w_in[:, :, 0:1408], kr_pad, w_in[:, :, 1440:1952]], axis=2).astype(BF16)
    bb, cc, ab = jax.vmap(jax.vmap(_s5_matrices))(s5_lam_re, s5_lam_im, s5_log_dt, s5_b_re, s5_b_im,
                                                  s5_c_re, s5_c_im)
    bd = jax.vmap(jax.vmap(_block_diag))
    wg = (0.5 * jnp.concatenate([bd(lru_wa), bd(lru_wx)], axis=-1)).astype(BF16)
    bg = 0.5 * jnp.concatenate([lru_ba, lru_bx], axis=-1)[:, :, None, :]
    sp = (0.5 * LRU_C) * jax.nn.softplus(-lru_lam)[:, :, None, :]
    w_uq_pad = _pad_heads(mla_w_uq.reshape(DEPTH, MLA_Q_RANK, MLA_HEADS, MLA_QK), MLA_QK).astype(BF16)
    ukv = mla_w_ukv.reshape(DEPTH, MLA_KV_RANK, MLA_HEADS, MLA_NOPE + MLA_V)
    w_ukv_pad = jnp.concatenate([_pad_heads(ukv[..., :MLA_NOPE], MLA_NOPE),
                                 _pad_heads(ukv[..., MLA_NOPE:], MLA_V)], axis=-1).astype(BF16)
    glu_w = s5_glu_w.astype(BF16)
    w_out_b = w_out.astype(BF16)
    row = lambda a: a[:, None, :]

    proj_w = (row(norm_g), w_in_pad)
    x_src = (ctx, x)
    (proj,) = _row_stage(0, x_src, mod, None, proj_w, None, merge=False, project=True)
    for l in range(DEPTH):
        final = l == DEPTH - 1
        y_fwd, y_bwd = (_scan(l, d, proj, bb, cc, ab, lru_conv_w, row(lru_conv_b), wg, bg, sp)
                        for d in range(2))
        ymla = _attention(l, proj, cosq, sinq, cosk, sink, row(mla_q_norm), w_uq_pad,
                          row(mla_kv_norm), w_ukv_pad, with_ctx=not final)
        merge_in = (y_fwd, y_bwd, proj, ymla, row(s5_d), glu_w, row(s5_glu_b), w_out_b)
        if final:
            (out,) = _row_stage(l, x_src, mod, merge_in, None, final_g[None], merge=True,
                                project=False, final=True)
            return out
        x_all, proj = _row_stage(l, x_src, mod, merge_in, proj_w, None, merge=True, project=True)
        x_src = (x_all,)
```

```python
import functools
import math

import jax
import jax.numpy as jnp
from jax import lax
from jax.experimental import pallas as pl
from jax.experimental.pallas import tpu as pltpu

F32 = jnp.float32
BF16 = jnp.bfloat16

D_MODEL = 1024
BATCH = 16
SEQ = 2048
DEPTH = 4
GRID_W = 64
CTX_LEN = 256
LT = CTX_LEN + SEQ
S5_WIDTH = 256
S5_GROUPS = 16
S5_STATE = 64
S5_NSTATE = S5_GROUPS * S5_STATE
LRU_WIDTH = 256
LRU_CONV = 4
LRU_C = 8.0
MLA_HEADS = 8
MLA_NOPE = 64
MLA_QK = 96
MLA_V = 64
MLA_WIDTH = 512
MLA_Q_RANK = 256
MLA_KV_RANK = 128
ATTN_SCALE = MLA_QK ** -0.5
ROPE_AXIS = 16
ROPE_BASE = 10000.0
EPS = 1e-6

LANES = 128
HEAD_PAD = LANES
QK_PAD = MLA_HEADS * HEAD_PAD
IN_PAD = 2048
MOD_ROWS = 24
ROW_BLK = 256
N_ROW_BLK = LT // ROW_BLK
TL = 64
NCT = CTX_LEN // TL
NLT = SEQ // TL
NCH = NCT + NLT
HALO = 16
CBW = 512
MXU_TILE = 256
SCORE_GROUP = 3

_VMEM_LIMIT = 48 * 1024 * 1024


def _split_bf16(v):
    hi = v.astype(BF16)
    lo = (v - hi.astype(F32)).astype(BF16)
    return hi, lo


def _sigmoid(v):
    return 0.5 * jnp.tanh(0.5 * v) + 0.5


def _silu(v):
    return v * _sigmoid(v)


def _rms(v, g):
    ms = jnp.mean(v * v, axis=-1, keepdims=True)
    return v * lax.rsqrt(ms + EPS) * g


def _layer_spec(l, shape):
    zeros = (0,) * len(shape)
    return pl.BlockSpec((None,) + shape, lambda a, b: (l,) + zeros)


def _mod_body(c_ref, w_ref, b_ref, o_ref):
    c = c_ref[...]
    sc = c * jax.nn.sigmoid(c)
    s_hi, s_lo = _split_bf16(sc)
    w_hi, w_lo = _split_bf16(w_ref[0])
    acc = jnp.dot(s_hi, w_hi, preferred_element_type=F32)
    acc += jnp.dot(s_hi, w_lo, preferred_element_type=F32)
    acc += jnp.dot(s_lo, w_hi, preferred_element_type=F32)
    o_ref[0] = acc + b_ref[0]


def _mod_tables(c_pad, ada_w, ada_b):
    nj = 3 * D_MODEL // 1024
    return pl.pallas_call(
        _mod_body,
        grid=(DEPTH, nj),
        in_specs=[
            pl.BlockSpec((MOD_ROWS, D_MODEL), lambda l, j: (0, 0)),
            pl.BlockSpec((1, D_MODEL, 1024), lambda l, j: (l, 0, j)),
            pl.BlockSpec((1, 1, 1024), lambda l, j: (l, 0, j)),
        ],
        out_specs=pl.BlockSpec((1, MOD_ROWS, 1024), lambda l, j: (l, 0, j)),
        out_shape=jax.ShapeDtypeStruct((DEPTH, MOD_ROWS, 3 * D_MODEL), F32),
        compiler_params=pltpu.CompilerParams(vmem_limit_bytes=_VMEM_LIMIT),
        name="adaln_tables",
    )(c_pad, ada_w, ada_b.reshape(DEPTH, 1, 3 * D_MODEL))


PROJ_US5, PROJ_XLRU, PROJ_ZLRU, PROJ_CQ, PROJ_KVR = 0, 2, 3, 4, 5


def _row_body(*refs, layer, merge, project, final, split_src):
    it = iter(refs)
    ctx_ref = next(it) if split_src else None
    x_ref, mod_ref = next(it), next(it)
    if merge:
        yf_ref, yb_ref, uz_ref, zlru_ref, ymla_ref, zmla_ref, d_ref, gw_ref, gb_ref, wo_ref = (
            next(it) for _ in range(10))
    fg_ref = next(it) if final else None
    if project:
        g_ref, wi_ref = next(it), next(it)
    outs = list(it)

    b = pl.program_id(0)
    t = pl.program_id(1) + (1 if final else 0)
    is_ctx = t == 0
    mod_row = jnp.where(is_ctx, BATCH, b)
    x = x_ref[0]
    if split_src:
        x = jnp.where(is_ctx, ctx_ref[0], x)

    if merge:
        gate = mod_ref[layer, pl.ds(mod_row, 1), 2 * D_MODEL:3 * D_MODEL]
        ysc = yf_ref[0].astype(F32) + yb_ref[0].astype(F32)
        uz = uz_ref[0].astype(F32)
        y = d_ref[...] * uz[:, 0:S5_WIDTH] + ysc[:, 0:S5_WIDTH]
        g = jnp.dot(y.astype(BF16), gw_ref[...], preferred_element_type=F32) + gb_ref[...]
        y_s5 = g[:, 0:S5_WIDTH] * _sigmoid(g[:, S5_WIDTH:])
        cat = jnp.concatenate([
            (y_s5 * _silu(uz[:, S5_WIDTH:])).astype(BF16),
            (ysc[:, S5_WIDTH:] * _silu(zlru_ref[0].astype(F32))).astype(BF16),
            (ymla_ref[0].astype(F32) * _silu(zmla_ref[0].astype(F32))).astype(BF16)], axis=-1)
        x = x + gate * jnp.dot(cat, wo_ref[...], preferred_element_type=F32)
        if final:
            outs.pop(0)[0] = _rms(x, fg_ref[...])
        else:
            outs.pop(0)[0] = x

    if project:
        nxt = layer + 1 if merge else layer
        shift = mod_ref[nxt, pl.ds(mod_row, 1), 0:D_MODEL]
        scale = mod_ref[nxt, pl.ds(mod_row, 1), D_MODEL:2 * D_MODEL]
        h = _rms(x, g_ref[...]) * (1.0 + scale) + shift
        outs.pop(0)[0] = jnp.dot(h.astype(BF16), wi_ref[...],
                                 preferred_element_type=F32).astype(BF16)


def _row_stage(layer, x_src, mod, merge_in, proj_w, final_g, *, merge, project, final=False):
    split_src = len(x_src) == 2
    t_off = 1 if final else 0
    row_spec = lambda w, cb=0: pl.BlockSpec((1, ROW_BLK, w), lambda b, t: (b, t + t_off, cb))
    const = lambda s: pl.BlockSpec(s, lambda b, t: (0,) * len(s))
    args, in_specs = [], []
    if split_src:
        args += list(x_src)
        in_specs += [pl.BlockSpec((1, ROW_BLK, D_MODEL), lambda b, t: (b, 0, 0)),
                     pl.BlockSpec((1, ROW_BLK, D_MODEL), lambda b, t: (b, jnp.maximum(t - 1, 0), 0))]
    else:
        args += list(x_src)
        in_specs += [row_spec(D_MODEL)]
    args.append(mod)
    in_specs.append(const((DEPTH, MOD_ROWS, 3 * D_MODEL)))
    if merge:
        y_fwd, y_bwd, proj, ymla, s5_d, glu_w, glu_b, w_out = merge_in
        args += [y_fwd, y_bwd, proj, proj, ymla, proj, s5_d, glu_w, glu_b, w_out]
        in_specs += [
            row_spec(512), row_spec(512),
            row_spec(512, 0),
            row_spec(256, PROJ_ZLRU),
            pl.BlockSpec((1, ROW_BLK, MLA_WIDTH), lambda b, t: (b, t, 0)),

            row_spec(512, 3),
            _layer_spec(layer, (1, S5_WIDTH)),
            _layer_spec(layer, (S5_WIDTH, 2 * S5_WIDTH)),
            _layer_spec(layer, (1, 2 * S5_WIDTH)),
            _layer_spec(layer, (D_MODEL, D_MODEL)),
        ]
    if final:
        args.append(final_g)
        in_specs.append(const((1, D_MODEL)))
    if project:
        nxt = layer + 1 if merge else layer
        norm_g, w_in_pad = proj_w
        args += [norm_g, w_in_pad]
        in_specs += [_layer_spec(nxt, (1, D_MODEL)), _layer_spec(nxt, (D_MODEL, IN_PAD))]

    out_shape, out_specs, aliases = [], [], {}
    if merge:
        if final:
            out_shape.append(jax.ShapeDtypeStruct((BATCH, SEQ, D_MODEL), F32))
            out_specs.append(pl.BlockSpec((1, ROW_BLK, D_MODEL), lambda b, t: (b, t, 0)))
        else:
            out_shape.append(jax.ShapeDtypeStruct((BATCH, LT, D_MODEL), F32))
            out_specs.append(row_spec(D_MODEL))
            if not split_src:
                aliases = {0: 0}
    if project:
        out_shape.append(jax.ShapeDtypeStruct((BATCH, LT, IN_PAD), BF16))
        out_specs.append(row_spec(IN_PAD))
    return pl.pallas_call(
        functools.partial(_row_body, layer=layer, merge=merge, project=project, final=final,
                          split_src=split_src),
        grid=(BATCH, N_ROW_BLK - t_off),
        in_specs=in_specs,
        out_specs=out_specs,
        out_shape=out_shape,
        input_output_aliases=aliases,
        compiler_params=pltpu.CompilerParams(vmem_limit_bytes=_VMEM_LIMIT),
        name="row_stage",
    )(*args)


def _chunk_index(bwd, i):
    if not bwd:
        return i
    return jnp.where(i < NCT, NCT - 1 - i, 2 * NCT + NLT - 1 - i)


def _time_major(v):
    t = v.shape[1]
    return jnp.swapaxes(v, 0, 1).reshape(t * BATCH, v.shape[2])


def _batch_major(v):
    return jnp.swapaxes(v.reshape(v.shape[0] // BATCH, BATCH, v.shape[1]), 0, 1)


def _scan_body(us5_ref, xl_ref, xp_ref, xn_ref, bb_ref, cc_ref, ab_ref, cw_ref, cb_ref,
               wg_ref, bg_ref, sp_ref, ysc_ref,
               xs, hb, xpad, la, lb, hs5, hlru, *, bwd):
    i = pl.program_id(0)
    c = _chunk_index(bwd, i)
    rows = TL * BATCH
    half = rows // 2

    @pl.when(i == 0)
    def _():
        hs5[...] = jnp.zeros_like(hs5)
        hlru[...] = jnp.zeros_like(hlru)

    def rows_of(k):
        t = TL - 1 - k if bwd else k
        return slice(t * BATCH, (t + 1) * BATCH)

    def half_rows(k):
        first = (1 - k) if bwd else k
        return slice(first * half, (first + 1) * half)

    prev_ok = jnp.where((c != 0) & (c != NCT), 1.0, 0.0).astype(F32)
    next_ok = jnp.where((c != NCT - 1) & (c != NCH - 1), 1.0, 0.0).astype(F32)
    xpad[0:2 * BATCH, :] = _time_major(xp_ref[...])[(HALO - 2) * BATCH:, :].astype(F32) * prev_ok
    xpad[pl.ds(2 * BATCH, rows), :] = _time_major(xl_ref[...]).astype(F32)
    xpad[pl.ds((TL + 2) * BATCH, BATCH), :] = _time_major(xn_ref[...])[0:BATCH, :].astype(F32) * next_ok

    def gates(r0):
        rs = slice(r0, r0 + ROW_BLK)
        xc = cb_ref[...]
        for k in range(LRU_CONV):
            xc = xc + xpad[r0 + k * BATCH:r0 + k * BATCH + ROW_BLK, :] * cw_ref[k:k + 1, :]
        th = jnp.tanh(jnp.dot(xc.astype(BF16), wg_ref[...], preferred_element_type=F32) + bg_ref[...])
        sp4 = sp_ref[...]
        neg_log_a = sp4 * th[:, 0:LRU_WIDTH] + sp4
        a = jnp.exp2(neg_log_a * (-math.log2(math.e)))
        z = jnp.maximum(jnp.tanh(neg_log_a) * (a * a + 1.0), 0.0)
        mult = jnp.where(z > 0.0, z * lax.rsqrt(z), 0.0)
        la[rs, :] = a
        lb[rs, :] = (mult * xc) * (0.5 * th[:, LRU_WIDTH:] + 0.5)

    ub = _time_major(us5_ref[...])
    blk_w = 2 * CBW
    n_blk = S5_NSTATE // CBW
    n_tiles = 2 * S5_NSTATE // MXU_TILE
    half_steps = TL // 2

    def b_tile(k, n):
        hr, ns = half_rows(k), slice(n * MXU_TILE, (n + 1) * MXU_TILE)
        xs[hr, ns] = jnp.dot(ub[hr], bb_ref[:, ns], preferred_element_type=F32)

    def load_carry(cb):
        st = slice(cb * CBW, (cb + 1) * CBW)
        return (hs5[0, :, st], hs5[1, :, st]) + ((hlru[...],) if cb == 0 else ())

    def save_carry(cb, carry):
        st = slice(cb * CBW, (cb + 1) * CBW)
        hs5[0, :, st] = carry[0]
        hs5[1, :, st] = carry[1]
        if cb == 0:
            hlru[...] = carry[2]

    def scan_steps(cb, carry, k0, k1):
        st = slice(cb * CBW, (cb + 1) * CBW)
        re = slice(cb * blk_w, cb * blk_w + CBW)
        im = slice(cb * blk_w + CBW, (cb + 1) * blk_w)
        ar = jnp.broadcast_to(ab_ref[0:1, st], (BATCH, CBW))
        ai = jnp.broadcast_to(ab_ref[1:2, st], (BATCH, CBW))
        for k in range(k0, k1):
            rs = rows_of(k)
            hr, hi = carry[0], carry[1]
            nr = ar * hr - ai * hi + xs[rs, re]
            ni = ar * hi + ai * hr + xs[rs, im]
            hb[rs, re] = nr.astype(BF16)
            hb[rs, im] = ni.astype(BF16)
            if cb == 0:
                hl = la[rs, :] * carry[2] + lb[rs, :]
                lb[rs, :] = hl
                carry = (nr, ni, hl)
            else:
                carry = (nr, ni)
        return carry

    def c_map(k):
        return jnp.dot(hb[half_rows(k), :], cc_ref[...], preferred_element_type=F32)

    gate_blocks = list(range(rows // ROW_BLK))
    if bwd:
        gate_blocks.reverse()
    for n in range(n_tiles):
        b_tile(0, n)
        if n % 2 == 0:
            gates(gate_blocks[n // 2] * ROW_BLK)
    tiles_per_blk = n_tiles // n_blk
    seg = half_steps // tiles_per_blk
    for cb in range(n_blk):
        carry = load_carry(cb)
        for n in range(tiles_per_blk):
            carry = scan_steps(cb, carry, n * seg, (n + 1) * seg)
            b_tile(1, cb * tiles_per_blk + n)
        save_carry(cb, carry)
    y_a = c_map(0)
    for cb in range(n_blk):
        save_carry(cb, scan_steps(cb, load_carry(cb), half_steps, TL))
    y_b = c_map(1)
    y = jnp.concatenate([y_b, y_a] if bwd else [y_a, y_b], axis=0)
    ysc_ref[:, :, 0:S5_WIDTH] = _batch_major(y.astype(ysc_ref.dtype))
    ysc_ref[:, :, S5_WIDTH:] = _batch_major(lb[...].astype(ysc_ref.dtype))


def _scan(l, d, proj, bb, cc, ab, conv_w, conv_b, wg, bg, sp):
    bwd = d == 1
    n_halo = LT // HALO
    chunk = lambda cb: (lambda i: (0, _chunk_index(bwd, i), cb))
    prev = lambda i: (0, jnp.maximum(_chunk_index(bwd, i) * (TL // HALO) - 1, 0), PROJ_XLRU)
    nxt = lambda i: (0, jnp.minimum((_chunk_index(bwd, i) + 1) * (TL // HALO), n_halo - 1), PROJ_XLRU)
    per_dir = lambda s: pl.BlockSpec((None, None) + s, lambda i: (l, d) + (0,) * len(s))
    per_layer = lambda s: pl.BlockSpec((None,) + s, lambda i: (l,) + (0,) * len(s))
    out_spec = pl.BlockSpec((BATCH, TL, 512), lambda i: (0, _chunk_index(bwd, i), 0))
    rows = TL * BATCH
    return pl.pallas_call(
        functools.partial(_scan_body, bwd=bwd),
        grid=(NCH,),
        in_specs=[
            pl.BlockSpec((BATCH, TL, S5_WIDTH), chunk(PROJ_US5)),
            pl.BlockSpec((BATCH, TL, LRU_WIDTH), chunk(PROJ_XLRU)),
            pl.BlockSpec((BATCH, HALO, LRU_WIDTH), prev),
            pl.BlockSpec((BATCH, HALO, LRU_WIDTH), nxt),
            per_dir((S5_WIDTH, 2 * S5_NSTATE)),
            per_dir((2 * S5_NSTATE, S5_WIDTH)),
            per_dir((2, S5_NSTATE)),
            per_layer((LRU_CONV, LRU_WIDTH)),
            per_layer((1, LRU_WIDTH)),
            per_dir((LRU_WIDTH, 2 * LRU_WIDTH)),
            per_dir((1, 2 * LRU_WIDTH)),
            per_dir((1, LRU_WIDTH)),
        ],
        out_specs=out_spec,
        out_shape=jax.ShapeDtypeStruct((BATCH, LT, 512), BF16),
        scratch_shapes=[
            pltpu.VMEM((rows, 2 * S5_NSTATE), F32),
            pltpu.VMEM((rows, 2 * S5_NSTATE), BF16),
            pltpu.VMEM(((TL + 3) * BATCH, LRU_WIDTH), F32),
            pltpu.VMEM((rows, LRU_WIDTH), F32),
            pltpu.VMEM((rows, LRU_WIDTH), F32),
            pltpu.VMEM((2, BATCH, S5_NSTATE), F32),
            pltpu.VMEM((BATCH, LRU_WIDTH), F32),
        ],
        compiler_params=pltpu.CompilerParams(vmem_limit_bytes=_VMEM_LIMIT),
        name="s5_lru_scan",
    )(proj, proj, proj, proj, bb, cc, ab, conv_w, conv_b, wg, bg, sp)


def _rope(v, cos, sin):
    width = v.shape[-1]
    lane = lax.broadcasted_iota(jnp.int32, v.shape, 1)
    first_half = ((lane // 8) % 2) == 0
    swapped = jnp.where(first_half, pltpu.roll(v, width - 8, 1), pltpu.roll(v, 8, 1))
    return v * cos + swapped * sin


def _attn_body(cq_ref, kvr_ref, cosq_ref, sinq_ref, cosk_ref, sink_ref, qn_ref, wuq_ref,
               kvn_ref, wukv_ref, o_ref, k_s, vt_s, *, q_off):
    j = pl.program_id(1)

    @pl.when(j == 0)
    def _prep():
        lane = lax.broadcasted_iota(jnp.int32, (ROW_BLK, HEAD_PAD), 1)
        ones_hi = jnp.where(lane >= MLA_V, 1.0, 0.0).astype(F32)

        def blk(r, carry):
            rs = pl.ds(pl.multiple_of(r * ROW_BLK, ROW_BLK), ROW_BLK)
            kvr = kvr_ref[0, rs, :].astype(F32)
            kvn = _rms(kvr[:, 0:MLA_KV_RANK], kvn_ref[...])
            kv = jnp.dot(kvn.astype(BF16), wukv_ref[...], preferred_element_type=F32)
            kro = _rope(kvr[:, MLA_KV_RANK:], cosk_ref[rs, :], sink_ref[rs, :])
            for h in range(MLA_HEADS):
                hs = slice(h * HEAD_PAD, (h + 1) * HEAD_PAD)
                k_s[rs, hs] = (kv[:, hs] + kro).astype(BF16)
                vt_s[r, hs, :] = (kv[:, QK_PAD + h * HEAD_PAD:QK_PAD + (h + 1) * HEAD_PAD]
                                  + ones_hi).T.astype(BF16)
            return carry
        lax.fori_loop(0, N_ROW_BLK, blk, 0)

    qn = _rms(cq_ref[0].astype(F32), qn_ref[...])
    q = jnp.dot(qn.astype(BF16), wuq_ref[...], preferred_element_type=F32)
    cosq = cosq_ref[...]
    sinq = sinq_ref[...]
    lane = lax.broadcasted_iota(jnp.int32, (ROW_BLK, HEAD_PAD), 1)

    def head_slice(h):
        return slice(h * HEAD_PAD, (h + 1) * HEAD_PAD)

    def roped_qt(h):
        return _rope(q[:, head_slice(h)], cosq, sinq).T.astype(BF16)

    def key_max(blocks):
        parts = [jnp.max(sb.reshape(sb.shape[0] // 8, 8, ROW_BLK), axis=0) for sb in blocks]
        mx = parts[0]
        for part in parts[1:]:
            mx = jnp.maximum(mx, part)
        return jnp.max(mx, axis=0, keepdims=True)

    def attend(n_blk):
        grp = SCORE_GROUP if n_blk % SCORE_GROUP == 0 else 1
        n_grp = n_blk // grp

        def score(qt, h, g):
            keys = k_s[g * grp * ROW_BLK:(g + 1) * grp * ROW_BLK, head_slice(h)]
            return jnp.dot(keys, qt, preferred_element_type=F32)

        q_cur = roped_qt(0)
        s_cur = [score(q_cur, 0, g) for g in range(n_grp)]
        m_cur = key_max(s_cur)
        pv = []
        for h in range(MLA_HEADS):
            more = h + 1 < MLA_HEADS
            if more:
                q_nxt = roped_qt(h + 1)
            p_blocks, s_nxt = [], []
            for g in range(n_grp):
                p_blocks.append(jnp.exp2(s_cur[g] - m_cur).astype(BF16))
                if more:
                    s_nxt.append(score(q_nxt, h + 1, g))
            p_t = jnp.concatenate(p_blocks, axis=0)
            v_t = jnp.concatenate([vt_s[r, head_slice(h), :] for r in range(n_blk)], axis=1)
            pv.append(jnp.dot(v_t, p_t, preferred_element_type=F32).T)
            if more:
                s_cur, m_cur = s_nxt, key_max(s_nxt)
            if h % 2 == 1:
                o_even = pv[h - 1] / pltpu.roll(pv[h - 1], MLA_V, 1)
                o_odd = pltpu.roll(pv[h], MLA_V, 1) / pv[h]
                o_ref[0, :, (h // 2) * LANES:(h // 2 + 1) * LANES] = jnp.where(
                    lane < MLA_V, o_even, o_odd).astype(o_ref.dtype)

    is_ctx = (j + q_off) == 0

    @pl.when(is_ctx)
    def _():
        attend(CTX_LEN // ROW_BLK)

    @pl.when(jnp.logical_not(is_ctx))
    def _():
        attend(N_ROW_BLK)


def _attention(l, proj, cosq, sinq, cosk, sink, q_norm, w_uq_pad, kv_norm, w_ukv_pad, with_ctx):
    q_off = 0 if with_ctx else 1
    const = lambda s: pl.BlockSpec(s, lambda b, j: (0,) * len(s))
    return pl.pallas_call(
        functools.partial(_attn_body, q_off=q_off),
        grid=(BATCH, N_ROW_BLK - q_off),
        in_specs=[
            pl.BlockSpec((1, ROW_BLK, MLA_Q_RANK), lambda b, j: (b, j + q_off, PROJ_CQ)),
            pl.BlockSpec((1, LT, 256), lambda b, j: (b, 0, PROJ_KVR)),
            pl.BlockSpec((ROW_BLK, HEAD_PAD), lambda b, j: (j + q_off, 0)),
            pl.BlockSpec((ROW_BLK, HEAD_PAD), lambda b, j: (j + q_off, 0)),
            const((LT, HEAD_PAD)),
            const((LT, HEAD_PAD)),
            _layer_spec(l, (1, MLA_Q_RANK)),
            _layer_spec(l, (MLA_Q_RANK, QK_PAD)),
            _layer_spec(l, (1, MLA_KV_RANK)),
            _layer_spec(l, (MLA_KV_RANK, 2 * QK_PAD)),
        ],
        out_specs=pl.BlockSpec((1, ROW_BLK, MLA_WIDTH), lambda b, j: (b, j, 0)),
        out_shape=jax.ShapeDtypeStruct((BATCH, LT - q_off * CTX_LEN, MLA_WIDTH), BF16),
        scratch_shapes=[
            pltpu.VMEM((LT, QK_PAD), BF16),
            pltpu.VMEM((N_ROW_BLK, QK_PAD, ROW_BLK), BF16),
        ],
        compiler_params=pltpu.CompilerParams(vmem_limit_bytes=_VMEM_LIMIT),
        name="mla_attention",
    )(proj, proj, cosq, sinq, cosk, sink, q_norm, w_uq_pad, kv_norm, w_ukv_pad)


def _block_diag(blocks):
    n, r, c = blocks.shape
    eye = jnp.eye(n, dtype=blocks.dtype)
    return (eye[:, None, :, None] * blocks[:, :, None, :]).reshape(n * r, n * c)


def _s5_matrices(lam_re, lam_im, log_dt, b_re, b_im, c_re, c_im):
    dt = jnp.exp(log_dt)[:, None]
    mag = jnp.exp(lam_re * dt)
    ab_re, ab_im = mag * jnp.cos(lam_im * dt), mag * jnp.sin(lam_im * dt)
    den = lam_re * lam_re + lam_im * lam_im
    nr = ab_re - 1.0
    coef_re = (nr * lam_re + ab_im * lam_im) / den
    coef_im = (ab_im * lam_re - nr * lam_im) / den
    bb_re = coef_re[..., None] * b_re - coef_im[..., None] * b_im
    bb_im = coef_re[..., None] * b_im + coef_im[..., None] * b_re
    to_in = lambda m: _block_diag(jnp.swapaxes(m, 1, 2))
    to_out = lambda m: _block_diag(jnp.swapaxes(m, 1, 2))

    def grouped(re, im, axis):
        parts = []
        for cb in range(S5_NSTATE // CBW):
            parts += [lax.slice_in_dim(re, cb * CBW, (cb + 1) * CBW, axis=axis),
                      lax.slice_in_dim(im, cb * CBW, (cb + 1) * CBW, axis=axis)]
        return jnp.concatenate(parts, axis=axis)

    bb = grouped(to_in(bb_re), to_in(bb_im), 1)
    cc = grouped(to_out(c_re), -to_out(c_im), 0)
    ab = jnp.stack([ab_re.reshape(-1), ab_im.reshape(-1)])
    return bb.astype(BF16), cc.astype(BF16), ab


def _rope_tables():
    n_rows = SEQ // GRID_W
    rows = jnp.repeat(jnp.arange(n_rows, dtype=F32), GRID_W)
    cols = jnp.tile(jnp.arange(GRID_W, dtype=F32), n_rows)
    inv_freq = ROPE_BASE ** (-jnp.arange(0, ROPE_AXIS, 2, dtype=F32) / ROPE_AXIS)
    ang_r = rows[:, None] * inv_freq
    ang_c = cols[:, None] * inv_freq
    cr, sr, cc, sc = jnp.cos(ang_r), jnp.sin(ang_r), jnp.cos(ang_c), jnp.sin(ang_c)
    cos32 = jnp.concatenate([cr, cr, cc, cc], axis=-1)
    sin32 = jnp.concatenate([-sr, sr, -sc, sc], axis=-1)
    cos = jnp.ones((LT, HEAD_PAD), F32).at[CTX_LEN:, MLA_NOPE:MLA_QK].set(cos32)
    sin = jnp.zeros((LT, HEAD_PAD), F32).at[CTX_LEN:, MLA_NOPE:MLA_QK].set(sin32)
    return cos, sin


def _pad_heads(w, used):
    w = jnp.pad(w, ((0, 0), (0, 0), (0, 0), (0, HEAD_PAD - used)))
    return w.reshape(w.shape[0], w.shape[1], MLA_HEADS * HEAD_PAD)


def kernel(x, c, ctx, c_ctx, ada_w, ada_b, norm_g, w_in, s5_lam_re, s5_lam_im, s5_log_dt, s5_b_re,
           s5_b_im, s5_c_re, s5_c_im, s5_d, s5_glu_w, s5_glu_b, lru_conv_w, lru_conv_b, lru_wa,
           lru_ba, lru_wx, lru_bx, lru_lam, mla_q_norm, mla_w_uq, mla_kv_norm, mla_w_ukv, w_out,
           final_g):
    c_pad = jnp.zeros((MOD_ROWS, D_MODEL), F32).at[:BATCH].set(c).at[BATCH].set(c_ctx)
    mod = _mod_tables(c_pad, ada_w, ada_b)

    cosk, sink = _rope_tables()
    q_mul = ATTN_SCALE * math.log2(math.e)
    cosq, sinq = cosk * q_mul, sink * q_mul

    kr_pad = jnp.zeros((DEPTH, D_MODEL, HEAD_PAD), F32).at[:, :, MLA_NOPE:MLA_QK].set(w_in[:, :, 1408:1440])
    w_in_pad = jnp.concatenate([w_in[:, :, 0:1408], kr_pad, w_in[:, :, 1440:1952]], axis=2).astype(BF16)
    bb, cc, ab = jax.vmap(jax.vmap(_s5_matrices))(s5_lam_re, s5_lam_im, s5_log_dt, s5_b_re, s5_b_im,
                                                  s5_c_re, s5_c_im)
    bd = jax.vmap(jax.vmap(_block_diag))
    wg = (0.5 * jnp.concatenate([bd(lru_wa), bd(lru_wx)], axis=-1)).astype(BF16)
    bg = 0.5 * jnp.concatenate([lru_ba, lru_bx], axis=-1)[:, :, None, :]
    sp = (0.5 * LRU_C) * jax.nn.softplus(-lru_lam)[:, :, None, :]
    w_uq_pad = _pad_heads(mla_w_uq.reshape(DEPTH, MLA_Q_RANK, MLA_HEADS, MLA_QK), MLA_QK).astype(BF16)
    ukv = mla_w_ukv.reshape(DEPTH, MLA_KV_RANK, MLA_HEADS, MLA_NOPE + MLA_V)
    w_ukv_pad = jnp.concatenate([_pad_heads(ukv[..., :MLA_NOPE], MLA_NOPE),
                                 _pad_heads(ukv[..., MLA_NOPE:], MLA_V)], axis=-1).astype(BF16)
    glu_w = s5_glu_w.astype(BF16)
    w_out_b = w_out.astype(BF16)
    row = lambda a: a[:, None, :]

    proj_w = (row(norm_g), w_in_pad)
    x_src = (ctx, x)
    (proj,) = _row_stage(0, x_src, mod, None, proj_w, None, merge=False, project=True)
    for l in range(DEPTH):
        final = l == DEPTH - 1
        y_fwd, y_bwd = (_scan(l, d, proj, bb, cc, ab, lru_conv_w, row(lru_conv_b), wg, bg, sp)
                        for d in range(2))
        ymla = _attention(l, proj, cosq, sinq, cosk, sink, row(mla_q_norm), w_uq_pad,
                          row(mla_kv_norm), w_ukv_pad, with_ctx=not final)
        merge_in = (y_fwd, y_bwd, proj, ymla, row(s5_d), glu_w, row(s5_glu_b), w_out_b)
        if final:
            (out,) = _row_stage(l, x_src, mod, merge_in, None, final_g[None], merge=True,
                                project=False, final=True)
            return out
        x_all, proj = _row_stage(l, x_src, mod, merge_in, proj_w, None, merge=True, project=True)
        x_src = (x_all,)
```

```python
import functools
import math

import jax
import jax.numpy as jnp
from jax import lax
from jax.experimental import pallas as pl
from jax.experimental.pallas import tpu as pltpu

F32 = jnp.float32
BF16 = jnp.bfloat16

D_MODEL = 1024
BATCH = 16
SEQ = 2048
DEPTH = 4
GRID_W = 64
CTX_LEN = 256
LT = CTX_LEN + SEQ
S5_WIDTH = 256
S5_GROUPS = 16
S5_STATE = 64
S5_NSTATE = S5_GROUPS * S5_STATE
LRU_WIDTH = 256
LRU_CONV = 4
LRU_C = 8.0
MLA_HEADS = 8
MLA_NOPE = 64
MLA_QK = 96
MLA_V = 64
MLA_WIDTH = 512
MLA_Q_RANK = 256
MLA_KV_RANK = 128
ATTN_SCALE = MLA_QK ** -0.5
ROPE_AXIS = 16
ROPE_BASE = 10000.0
EPS = 1e-6

LANES = 128
HEAD_PAD = LANES
QK_PAD = MLA_HEADS * HEAD_PAD
IN_PAD = 2048
MOD_ROWS = 24
ROW_BLK = 256
ROW_PAR = 2
N_ROW_BLK = LT // ROW_BLK
TL = 64
NCT = CTX_LEN // TL
NLT = SEQ // TL
NCH = NCT + NLT
HALO = 16
CBW = 512
MXU_TILE = 256

_VMEM_LIMIT = 48 * 1024 * 1024


def _split_bf16(v):
    hi = v.astype(BF16)
    lo = (v - hi.astype(F32)).astype(BF16)
    return hi, lo


def _sigmoid(v):
    return 0.5 * jnp.tanh(0.5 * v) + 0.5


def _silu(v):
    return v * _sigmoid(v)


def _rms(v, g):
    ms = jnp.mean(v * v, axis=-1, keepdims=True)
    return v * lax.rsqrt(ms + EPS) * g


def _layer_spec(l, shape):
    zeros = (0,) * len(shape)
    return pl.BlockSpec((None,) + shape, lambda a, b: (l,) + zeros)


def _mod_body(c_ref, w_ref, b_ref, o_ref):
    c = c_ref[...]
    sc = c * jax.nn.sigmoid(c)
    s_hi, s_lo = _split_bf16(sc)
    w_hi, w_lo = _split_bf16(w_ref[0])
    acc = jnp.dot(s_hi, w_hi, preferred_element_type=F32)
    acc += jnp.dot(s_hi, w_lo, preferred_element_type=F32)
    acc += jnp.dot(s_lo, w_hi, preferred_element_type=F32)
    o_ref[0] = acc + b_ref[0]


def _mod_tables(c_pad, ada_w, ada_b):
    nj = 3 * D_MODEL // 1024
    return pl.pallas_call(
        _mod_body,
        grid=(DEPTH, nj),
        in_specs=[
            pl.BlockSpec((MOD_ROWS, D_MODEL), lambda l, j: (0, 0)),
            pl.BlockSpec((1, D_MODEL, 1024), lambda l, j: (l, 0, j)),
            pl.BlockSpec((1, 1, 1024), lambda l, j: (l, 0, j)),
        ],
        out_specs=pl.BlockSpec((1, MOD_ROWS, 1024), lambda l, j: (l, 0, j)),
        out_shape=jax.ShapeDtypeStruct((DEPTH, MOD_ROWS, 3 * D_MODEL), F32),
        compiler_params=pltpu.CompilerParams(vmem_limit_bytes=_VMEM_LIMIT),
        name="adaln_tables",
    )(c_pad, ada_w, ada_b.reshape(DEPTH, 1, 3 * D_MODEL))


PROJ_US5, PROJ_XLRU, PROJ_ZLRU, PROJ_CQ, PROJ_KVR = 0, 2, 3, 4, 5


def _row_body(*refs, layer, merge, project, final, split_src):
    it = iter(refs)
    ctx_ref = next(it) if split_src else None
    x_ref, mod_ref = next(it), next(it)
    if merge:
        yf_ref, yb_ref, uz_ref, zlru_ref, ymla_ref, zmla_ref, d_ref, gw_ref, gb_ref, wo_ref = (
            next(it) for _ in range(10))
    fg_ref = next(it) if final else None
    if project:
        g_ref, wi_ref = next(it), next(it)
    outs = list(it)

    b = pl.program_id(0)
    t = pl.program_id(1) + (1 if final else 0)
    is_ctx = t == 0
    rows = ROW_PAR * ROW_BLK

    def flat(ref):
        return ref[...].reshape(rows, ref.shape[-1])

    def mod_vec(layer_, p, lo):
        mod_row = jnp.where(is_ctx, BATCH, b * ROW_PAR + p)
        return mod_ref[layer_, pl.ds(mod_row, 1), lo:lo + D_MODEL]

    def per_sample(v, fn):
        return jnp.concatenate([fn(v[p * ROW_BLK:(p + 1) * ROW_BLK], p) for p in range(ROW_PAR)], axis=0)

    x = flat(x_ref)
    if split_src:
        x = jnp.where(is_ctx, flat(ctx_ref), x)

    if merge:
        ysc = flat(yf_ref).astype(F32) + flat(yb_ref).astype(F32)
        uz = flat(uz_ref).astype(F32)
        y = d_ref[...] * uz[:, 0:S5_WIDTH] + ysc[:, 0:S5_WIDTH]
        g = jnp.dot(y.astype(BF16), gw_ref[...], preferred_element_type=F32) + gb_ref[...]
        y_s5 = g[:, 0:S5_WIDTH] * _sigmoid(g[:, S5_WIDTH:])
        cat = jnp.concatenate([
            (y_s5 * _silu(uz[:, S5_WIDTH:])).astype(BF16),
            (ysc[:, S5_WIDTH:] * _silu(flat(zlru_ref).astype(F32))).astype(BF16),
            (flat(ymla_ref).astype(F32) * _silu(flat(zmla_ref).astype(F32))).astype(BF16)], axis=-1)
        o = jnp.dot(cat, wo_ref[...], preferred_element_type=F32)
        x = x + per_sample(o, lambda v, p: mod_vec(layer, p, 2 * D_MODEL) * v)
        out = _rms(x, fg_ref[...]) if final else x
        out_ref = outs.pop(0)
        out_ref[...] = out.reshape(out_ref.shape)

    if project:
        nxt = layer + 1 if merge else layer
        h = per_sample(_rms(x, g_ref[...]),
                       lambda v, p: v * (1.0 + mod_vec(nxt, p, D_MODEL)) + mod_vec(nxt, p, 0))
        proj_ref = outs.pop(0)
        proj_ref[...] = jnp.dot(h.astype(BF16), wi_ref[...],
                                preferred_element_type=F32).astype(BF16).reshape(proj_ref.shape)


def _row_stage(layer, x_src, mod, merge_in, proj_w, final_g, *, merge, project, final=False):
    split_src = len(x_src) == 2
    t_off = 1 if final else 0
    row_spec = lambda w, cb=0: pl.BlockSpec((ROW_PAR, ROW_BLK, w), lambda b, t: (b, t + t_off, cb))
    const = lambda s: pl.BlockSpec(s, lambda b, t: (0,) * len(s))
    args, in_specs = [], []
    if split_src:
        args += list(x_src)
        in_specs += [pl.BlockSpec((ROW_PAR, ROW_BLK, D_MODEL), lambda b, t: (b, 0, 0)),
                     pl.BlockSpec((ROW_PAR, ROW_BLK, D_MODEL), lambda b, t: (b, jnp.maximum(t - 1, 0), 0))]
    else:
        args += list(x_src)
        in_specs += [row_spec(D_MODEL)]
    args.append(mod)
    in_specs.append(const((DEPTH, MOD_ROWS, 3 * D_MODEL)))
    if merge:
        y_fwd, y_bwd, proj, ymla, s5_d, glu_w, glu_b, w_out = merge_in
        args += [y_fwd, y_bwd, proj, proj, ymla, proj, s5_d, glu_w, glu_b, w_out]
        in_specs += [
            row_spec(512), row_spec(512),
            row_spec(512, 0),
            row_spec(256, PROJ_ZLRU),
            pl.BlockSpec((ROW_PAR, ROW_BLK, MLA_WIDTH), lambda b, t: (b, t, 0)),
            row_spec(512, 3),
            _layer_spec(layer, (1, S5_WIDTH)),
            _layer_spec(layer, (S5_WIDTH, 2 * S5_WIDTH)),
            _layer_spec(layer, (1, 2 * S5_WIDTH)),
            _layer_spec(layer, (D_MODEL, D_MODEL)),
        ]
    if final:
        args.append(final_g)
        in_specs.append(const((1, D_MODEL)))
    if project:
        nxt = layer + 1 if merge else layer
        norm_g, w_in_pad = proj_w
        args += [norm_g, w_in_pad]
        in_specs += [_layer_spec(nxt, (1, D_MODEL)), _layer_spec(nxt, (D_MODEL, IN_PAD))]

    out_shape, out_specs, aliases = [], [], {}
    if merge:
        if final:
            out_shape.append(jax.ShapeDtypeStruct((BATCH, SEQ, D_MODEL), F32))
            out_specs.append(pl.BlockSpec((ROW_PAR, ROW_BLK, D_MODEL), lambda b, t: (b, t, 0)))
        else:
            out_shape.append(jax.ShapeDtypeStruct((BATCH, LT, D_MODEL), F32))
            out_specs.append(row_spec(D_MODEL))
            if not split_src:
                aliases = {0: 0}
    if project:
        out_shape.append(jax.ShapeDtypeStruct((BATCH, LT, IN_PAD), BF16))
        out_specs.append(row_spec(IN_PAD))
    return pl.pallas_call(
        functools.partial(_row_body, layer=layer, merge=merge, project=project, final=final,
                          split_src=split_src),
        grid=(BATCH // ROW_PAR, N_ROW_BLK - t_off),
        in_specs=in_specs,
        out_specs=out_specs,
        out_shape=out_shape,
        input_output_aliases=aliases,
        compiler_params=pltpu.CompilerParams(vmem_limit_bytes=_VMEM_LIMIT),
        name="row_stage",
    )(*args)


def _chunk_index(bwd, i):
    if not bwd:
        return i
    return jnp.where(i < NCT, NCT - 1 - i, 2 * NCT + NLT - 1 - i)


def _time_major(v):
    t = v.shape[1]
    return jnp.swapaxes(v, 0, 1).reshape(t * BATCH, v.shape[2])


def _batch_major(v):
    return jnp.swapaxes(v.reshape(v.shape[0] // BATCH, BATCH, v.shape[1]), 0, 1)


def _scan_body(us5_ref, xl_ref, xp_ref, xn_ref, bb_ref, cc_ref, ab_ref, cw_ref, cb_ref,
               wg_ref, bg_ref, sp_ref, ysc_ref,
               xs, hb, xpad, la, lb, hs5, hlru, *, bwd):
    i = pl.program_id(0)
    c = _chunk_index(bwd, i)
    rows = TL * BATCH
    half = rows // 2

    @pl.when(i == 0)
    def _():
        hs5[...] = jnp.zeros_like(hs5)
        hlru[...] = jnp.zeros_like(hlru)

    def rows_of(k):
        t = TL - 1 - k if bwd else k
        return slice(t * BATCH, (t + 1) * BATCH)

    def half_rows(k):
        first = (1 - k) if bwd else k
        return slice(first * half, (first + 1) * half)

    prev_ok = jnp.where((c != 0) & (c != NCT), 1.0, 0.0).astype(F32)
    next_ok = jnp.where((c != NCT - 1) & (c != NCH - 1), 1.0, 0.0).astype(F32)
    xpad[0:2 * BATCH, :] = _time_major(xp_ref[...])[(HALO - 2) * BATCH:, :].astype(F32) * prev_ok
    xpad[pl.ds(2 * BATCH, rows), :] = _time_major(xl_ref[...]).astype(F32)
    xpad[pl.ds((TL + 2) * BATCH, BATCH), :] = _time_major(xn_ref[...])[0:BATCH, :].astype(F32) * next_ok

    def gates(r0):
        rs = slice(r0, r0 + ROW_BLK)
        xc = cb_ref[...]
        for k in range(LRU_CONV):
            xc = xc + xpad[r0 + k * BATCH:r0 + k * BATCH + ROW_BLK, :] * cw_ref[k:k + 1, :]
        th = jnp.tanh(jnp.dot(xc.astype(BF16), wg_ref[...], preferred_element_type=F32) + bg_ref[...])
        sp4 = sp_ref[...]
        neg_log_a = sp4 * th[:, 0:LRU_WIDTH] + sp4
        a = jnp.exp2(neg_log_a * (-math.log2(math.e)))
        z = jnp.maximum(jnp.tanh(neg_log_a) * (a * a + 1.0), 0.0)
        mult = jnp.where(z > 0.0, z * lax.rsqrt(z), 0.0)
        la[rs, :] = a
        lb[rs, :] = (mult * xc) * (0.5 * th[:, LRU_WIDTH:] + 0.5)

    ub = _time_major(us5_ref[...])
    blk_w = 2 * CBW
    n_blk = S5_NSTATE // CBW
    n_tiles = 2 * S5_NSTATE // MXU_TILE
    half_steps = TL // 2

    def b_tile(k, n):
        hr, ns = half_rows(k), slice(n * MXU_TILE, (n + 1) * MXU_TILE)
        xs[hr, ns] = jnp.dot(ub[hr], bb_ref[:, ns], preferred_element_type=F32)

    def load_carry(cb):
        st = slice(cb * CBW, (cb + 1) * CBW)
        return (hs5[0, :, st], hs5[1, :, st]) + ((hlru[...],) if cb == 0 else ())

    def save_carry(cb, carry):
        st = slice(cb * CBW, (cb + 1) * CBW)
        hs5[0, :, st] = carry[0]
        hs5[1, :, st] = carry[1]
        if cb == 0:
            hlru[...] = carry[2]

    def scan_steps(cb, carry, k0, k1):
        st = slice(cb * CBW, (cb + 1) * CBW)
        re = slice(cb * blk_w, cb * blk_w + CBW)
        im = slice(cb * blk_w + CBW, (cb + 1) * blk_w)
        ar = jnp.broadcast_to(ab_ref[0:1, st], (BATCH, CBW))
        ai = jnp.broadcast_to(ab_ref[1:2, st], (BATCH, CBW))
        for k in range(k0, k1):
            rs = rows_of(k)
            hr, hi = carry[0], carry[1]
            nr = ar * hr - ai * hi + xs[rs, re]
            ni = ar * hi + ai * hr + xs[rs, im]
            hb[rs, re] = nr.astype(BF16)
            hb[rs, im] = ni.astype(BF16)
            if cb == 0:
                hl = la[rs, :] * carry[2] + lb[rs, :]
                lb[rs, :] = hl
                carry = (nr, ni, hl)
            else:
                carry = (nr, ni)
        return carry

    def c_map(k):
        return jnp.dot(hb[half_rows(k), :], cc_ref[...], preferred_element_type=F32)

    gate_blocks = list(range(rows // ROW_BLK))
    if bwd:
        gate_blocks.reverse()
    for n in range(n_tiles):
        b_tile(0, n)
        if n % 2 == 0:
            gates(gate_blocks[n // 2] * ROW_BLK)
    tiles_per_blk = n_tiles // n_blk
    seg = half_steps // tiles_per_blk
    for cb in range(n_blk):
        carry = load_carry(cb)
        for n in range(tiles_per_blk):
            carry = scan_steps(cb, carry, n * seg, (n + 1) * seg)
            b_tile(1, cb * tiles_per_blk + n)
        save_carry(cb, carry)
    y_a = c_map(0)
    for cb in range(n_blk):
        save_carry(cb, scan_steps(cb, load_carry(cb), half_steps, TL))
    y_b = c_map(1)
    y = jnp.concatenate([y_b, y_a] if bwd else [y_a, y_b], axis=0)
    ysc_ref[:, :, 0:S5_WIDTH] = _batch_major(y.astype(ysc_ref.dtype))
    ysc_ref[:, :, S5_WIDTH:] = _batch_major(lb[...].astype(ysc_ref.dtype))


def _scan(l, d, proj, bb, cc, ab, conv_w, conv_b, wg, bg, sp):
    bwd = d == 1
    n_halo = LT // HALO
    chunk = lambda cb: (lambda i: (0, _chunk_index(bwd, i), cb))
    prev = lambda i: (0, jnp.maximum(_chunk_index(bwd, i) * (TL // HALO) - 1, 0), PROJ_XLRU)
    nxt = lambda i: (0, jnp.minimum((_chunk_index(bwd, i) + 1) * (TL // HALO), n_halo - 1), PROJ_XLRU)
    per_dir = lambda s: pl.BlockSpec((None, None) + s, lambda i: (l, d) + (0,) * len(s))
    per_layer = lambda s: pl.BlockSpec((None,) + s, lambda i: (l,) + (0,) * len(s))
    out_spec = pl.BlockSpec((BATCH, TL, 512), lambda i: (0, _chunk_index(bwd, i), 0))
    rows = TL * BATCH
    return pl.pallas_call(
        functools.partial(_scan_body, bwd=bwd),
        grid=(NCH,),
        in_specs=[
            pl.BlockSpec((BATCH, TL, S5_WIDTH), chunk(PROJ_US5)),
            pl.BlockSpec((BATCH, TL, LRU_WIDTH), chunk(PROJ_XLRU)),
            pl.BlockSpec((BATCH, HALO, LRU_WIDTH), prev),
            pl.BlockSpec((BATCH, HALO, LRU_WIDTH), nxt),
            per_dir((S5_WIDTH, 2 * S5_NSTATE)),
            per_dir((2 * S5_NSTATE, S5_WIDTH)),
            per_dir((2, S5_NSTATE)),
            per_layer((LRU_CONV, LRU_WIDTH)),
            per_layer((1, LRU_WIDTH)),
            per_dir((LRU_WIDTH, 2 * LRU_WIDTH)),
            per_dir((1, 2 * LRU_WIDTH)),
            per_dir((1, LRU_WIDTH)),
        ],
        out_specs=out_spec,
        out_shape=jax.ShapeDtypeStruct((BATCH, LT, 512), BF16),
        scratch_shapes=[
            pltpu.VMEM((rows, 2 * S5_NSTATE), F32),
            pltpu.VMEM((rows, 2 * S5_NSTATE), BF16),
            pltpu.VMEM(((TL + 3) * BATCH, LRU_WIDTH), F32),
            pltpu.VMEM((rows, LRU_WIDTH), F32),
            pltpu.VMEM((rows, LRU_WIDTH), F32),
            pltpu.VMEM((2, BATCH, S5_NSTATE), F32),
            pltpu.VMEM((BATCH, LRU_WIDTH), F32),
        ],
        compiler_params=pltpu.CompilerParams(vmem_limit_bytes=_VMEM_LIMIT),
        name="s5_lru_scan",
    )(proj, proj, proj, proj, bb, cc, ab, conv_w, conv_b, wg, bg, sp)


def _rope(v, cos, sin):
    width = v.shape[-1]
    lane = lax.broadcasted_iota(jnp.int32, v.shape, 1)
    first_half = ((lane // 8) % 2) == 0
    swapped = jnp.where(first_half, pltpu.roll(v, width - 8, 1), pltpu.roll(v, 8, 1))
    return v * cos + swapped * sin


def _attn_body(cq_ref, kvr_ref, cosq_ref, sinq_ref, cosk_ref, sink_ref, qn_ref, wuq_ref,
               kvn_ref, wukv_ref, o_ref, kt_s, v_s, *, q_off):
    j = pl.program_id(1)

    @pl.when(j == 0)
    def _prep():
        lane = lax.broadcasted_iota(jnp.int32, (ROW_BLK, QK_PAD), 1)
        ones_hi = jnp.where(lane % HEAD_PAD >= MLA_V, 1.0, 0.0).astype(F32)

        def blk(r, carry):
            rs = pl.ds(pl.multiple_of(r * ROW_BLK, ROW_BLK), ROW_BLK)
            kvr = kvr_ref[0, rs, :].astype(F32)
            kvn = _rms(kvr[:, 0:MLA_KV_RANK], kvn_ref[...])
            kv = jnp.dot(kvn.astype(BF16), wukv_ref[...], preferred_element_type=F32)
            kro = _rope(kvr[:, MLA_KV_RANK:], cosk_ref[rs, :], sink_ref[rs, :])
            for h in range(MLA_HEADS):
                hs = slice(h * HEAD_PAD, (h + 1) * HEAD_PAD)
                kt_s[r, hs, :] = (kv[:, hs] + kro).T.astype(BF16)
            v_s[rs, :] = (kv[:, QK_PAD:] + ones_hi).astype(BF16)
            return carry
        lax.fori_loop(0, N_ROW_BLK, blk, 0)

    qn = _rms(cq_ref[0].astype(F32), qn_ref[...])
    q = jnp.dot(qn.astype(BF16), wuq_ref[...], preferred_element_type=F32)
    cosq = cosq_ref[...]
    sinq = sinq_ref[...]
    lane = lax.broadcasted_iota(jnp.int32, (ROW_BLK, HEAD_PAD), 1)

    def head_slice(h):
        return slice(h * HEAD_PAD, (h + 1) * HEAD_PAD)

    def roped_q(h):
        return _rope(q[:, head_slice(h)], cosq, sinq).astype(BF16)

    def row_max(blocks):
        mx = blocks[0]
        for sb in blocks[1:]:
            mx = jnp.maximum(mx, sb)
        return jnp.max(mx, axis=-1, keepdims=True)

    def attend(n_blk):
        score = lambda qh, h, r: jnp.dot(qh, kt_s[r, head_slice(h), :], preferred_element_type=F32)
        q_cur = roped_q(0)
        s_cur = [score(q_cur, 0, r) for r in range(n_blk)]
        m_cur = row_max(s_cur)
        pv = []
        for h in range(MLA_HEADS):
            more = h + 1 < MLA_HEADS
            if more:
                q_nxt = roped_q(h + 1)
            p_blocks, s_nxt = [], []
            for r in range(n_blk):
                p_blocks.append(jnp.exp2(s_cur[r] - m_cur).astype(BF16))
                if more:
                    s_nxt.append(score(q_nxt, h + 1, r))
            p = jnp.concatenate(p_blocks, axis=-1)
            pv.append(jnp.dot(p, v_s[0:n_blk * ROW_BLK, head_slice(h)], preferred_element_type=F32))
            if more:
                s_cur, m_cur = s_nxt, row_max(s_nxt)
            if h % 2 == 1:
                o_even = pv[h - 1] / pltpu.roll(pv[h - 1], MLA_V, 1)
                o_odd = pltpu.roll(pv[h], MLA_V, 1) / pv[h]
                o_ref[0, :, (h // 2) * LANES:(h // 2 + 1) * LANES] = jnp.where(
                    lane < MLA_V, o_even, o_odd).astype(o_ref.dtype)

    is_ctx = (j + q_off) == 0

    @pl.when(is_ctx)
    def _():
        attend(CTX_LEN // ROW_BLK)

    @pl.when(jnp.logical_not(is_ctx))
    def _():
        attend(N_ROW_BLK)


def _attention(l, proj, cosq, sinq, cosk, sink, q_norm, w_uq_pad, kv_norm, w_ukv_pad, with_ctx):
    q_off = 0 if with_ctx else 1
    const = lambda s: pl.BlockSpec(s, lambda b, j: (0,) * len(s))
    return pl.pallas_call(
        functools.partial(_attn_body, q_off=q_off),
        grid=(BATCH, N_ROW_BLK - q_off),
        in_specs=[
            pl.BlockSpec((1, ROW_BLK, MLA_Q_RANK), lambda b, j: (b, j + q_off, PROJ_CQ)),
            pl.BlockSpec((1, LT, 256), lambda b, j: (b, 0, PROJ_KVR)),
            pl.BlockSpec((ROW_BLK, HEAD_PAD), lambda b, j: (j + q_off, 0)),
            pl.BlockSpec((ROW_BLK, HEAD_PAD), lambda b, j: (j + q_off, 0)),
            const((LT, HEAD_PAD)),
            const((LT, HEAD_PAD)),
            _layer_spec(l, (1, MLA_Q_RANK)),
            _layer_spec(l, (MLA_Q_RANK, QK_PAD)),
            _layer_spec(l, (1, MLA_KV_RANK)),
            _layer_spec(l, (MLA_KV_RANK, 2 * QK_PAD)),
        ],
        out_specs=pl.BlockSpec((1, ROW_BLK, MLA_WIDTH), lambda b, j: (b, j, 0)),
        out_shape=jax.ShapeDtypeStruct((BATCH, LT - q_off * CTX_LEN, MLA_WIDTH), BF16),
        scratch_shapes=[
            pltpu.VMEM((N_ROW_BLK, QK_PAD, ROW_BLK), BF16),
            pltpu.VMEM((LT, QK_PAD), BF16),
        ],
        compiler_params=pltpu.CompilerParams(vmem_limit_bytes=_VMEM_LIMIT),
        name="mla_attention",
    )(proj, proj, cosq, sinq, cosk, sink, q_norm, w_uq_pad, kv_norm, w_ukv_pad)


def _block_diag(blocks):
    n, r, c = blocks.shape
    tiled = jnp.tile(blocks.reshape(n * r, c), (1, n))
    on_diag = (jnp.arange(n * r)[:, None] // r) == (jnp.arange(n * c)[None, :] // c)
    return jnp.where(on_diag, tiled, jnp.zeros_like(tiled))


def _s5_matrices(lam_re, lam_im, log_dt, b_re, b_im, c_re, c_im):
    dt = jnp.exp(log_dt)[:, None]
    mag = jnp.exp(lam_re * dt)
    ab_re, ab_im = mag * jnp.cos(lam_im * dt), mag * jnp.sin(lam_im * dt)
    den = lam_re * lam_re + lam_im * lam_im
    nr = ab_re - 1.0
    coef_re = (nr * lam_re + ab_im * lam_im) / den
    coef_im = (ab_im * lam_re - nr * lam_im) / den
    bb_re = coef_re[..., None] * b_re - coef_im[..., None] * b_im
    bb_im = coef_re[..., None] * b_im + coef_im[..., None] * b_re
    to_in = lambda m: _block_diag(jnp.swapaxes(m, 1, 2))
    to_out = lambda m: _block_diag(jnp.swapaxes(m, 1, 2))

    def grouped(re, im, axis):
        parts = []
        for cb in range(S5_NSTATE // CBW):
            parts += [lax.slice_in_dim(re, cb * CBW, (cb + 1) * CBW, axis=axis),
                      lax.slice_in_dim(im, cb * CBW, (cb + 1) * CBW, axis=axis)]
        return jnp.concatenate(parts, axis=axis)

    bb = grouped(to_in(bb_re), to_in(bb_im), 1)
    cc = grouped(to_out(c_re), -to_out(c_im), 0)
    ab = jnp.stack([ab_re.reshape(-1), ab_im.reshape(-1)])
    return bb.astype(BF16), cc.astype(BF16), ab


def _rope_tables():
    n_rows = SEQ // GRID_W
    rows = jnp.repeat(jnp.arange(n_rows, dtype=F32), GRID_W)
    cols = jnp.tile(jnp.arange(GRID_W, dtype=F32), n_rows)
    inv_freq = ROPE_BASE ** (-jnp.arange(0, ROPE_AXIS, 2, dtype=F32) / ROPE_AXIS)
    ang_r = rows[:, None] * inv_freq
    ang_c = cols[:, None] * inv_freq
    cr, sr, cc, sc = jnp.cos(ang_r), jnp.sin(ang_r), jnp.cos(ang_c), jnp.sin(ang_c)
    cos32 = jnp.concatenate([cr, cr, cc, cc], axis=-1)
    sin32 = jnp.concatenate([-sr, sr, -sc, sc], axis=-1)
    cos = jnp.ones((LT, HEAD_PAD), F32).at[CTX_LEN:, MLA_NOPE:MLA_QK].set(cos32)
    sin = jnp.zeros((LT, HEAD_PAD), F32).at[CTX_LEN:, MLA_NOPE:MLA_QK].set(sin32)
    return cos, sin


def _pad_heads(w, used):
    w = jnp.pad(w, ((0, 0), (0, 0), (0, 0), (0, HEAD_PAD - used)))
    return w.reshape(w.shape[0], w.shape[1], MLA_HEADS * HEAD_PAD)


def kernel(x, c, ctx, c_ctx, ada_w, ada_b, norm_g, w_in, s5_lam_re, s5_lam_im, s5_log_dt, s5_b_re,
           s5_b_im, s5_c_re, s5_c_im, s5_d, s5_glu_w, s5_glu_b, lru_conv_w, lru_conv_b, lru_wa,
           lru_ba, lru_wx, lru_bx, lru_lam, mla_q_norm, mla_w_uq, mla_kv_norm, mla_w_ukv, w_out,
           final_g):
    c_pad = jnp.zeros((MOD_ROWS, D_MODEL), F32).at[:BATCH].set(c).at[BATCH].set(c_ctx)
    mod = _mod_tables(c_pad, ada_w, ada_b)

    cosk, sink = _rope_tables()
    q_mul = ATTN_SCALE * math.log2(math.e)
    cosq, sinq = cosk * q_mul, sink * q_mul

    kr_pad = jnp.zeros((DEPTH, D_MODEL, HEAD_PAD), F32).at[:, :, MLA_NOPE:MLA_QK].set(w_in[:, :, 1408:1440])
    w_in_pad = jnp.concatenate([w_in[:, :, 0:1408], kr_pad, w_in[:, :, 1440:1952]], axis=2).astype(BF16)
    bb, cc, ab = jax.vmap(jax.vmap(_s5_matrices))(s5_lam_re, s5_lam_im, s5_log_dt, s5_b_re, s5_b_im,
                                                  s5_c_re, s5_c_im)
    bd = jax.vmap(jax.vmap(_block_diag))
    wg = (0.5 * jnp.concatenate([bd(lru_wa), bd(lru_wx)], axis=-1)).astype(BF16)
    bg = 0.5 * jnp.concatenate([lru_ba, lru_bx], axis=-1)[:, :, None, :]
    sp = (0.5 * LRU_C) * jax.nn.softplus(-lru_lam)[:, :, None, :]
    w_uq_pad = _pad_heads(mla_w_uq.reshape(DEPTH, MLA_Q_RANK, MLA_HEADS, MLA_QK), MLA_QK).astype(BF16)
    ukv = mla_w_ukv.reshape(DEPTH, MLA_KV_RANK, MLA_HEADS, MLA_NOPE + MLA_V)
    w_ukv_pad = jnp.concatenate([_pad_heads(ukv[..., :MLA_NOPE], MLA_NOPE),
                                 _pad_heads(ukv[..., MLA_NOPE:], MLA_V)], axis=-1).astype(BF16)
    glu_w = s5_glu_w.astype(BF16)
    w_out_b = w_out.astype(BF16)
    row = lambda a: a[:, None, :]

    proj_w = (row(norm_g), w_in_pad)
    x_src = (ctx, x)
    (proj,) = _row_stage(0, x_src, mod, None, proj_w, None, merge=False, project=True)
    for l in range(DEPTH):
        final = l == DEPTH - 1
        y_fwd, y_bwd = (_scan(l, d, proj, bb, cc, ab, lru_conv_w, row(lru_conv_b), wg, bg, sp)
                        for d in range(2))
        ymla = _attention(l, proj, cosq, sinq, cosk, sink, row(mla_q_norm), w_uq_pad,
                          row(mla_kv_norm), w_ukv_pad, with_ctx=not final)
        merge_in = (y_fwd, y_bwd, proj, ymla, row(s5_d), glu_w, row(s5_glu_b), w_out_b)
        if final:
            (out,) = _row_stage(l, x_src, mod, merge_in, None, final_g[None], merge=True,
                                project=False, final=True)
            return out
        x_all, proj = _row_stage(l, x_src, mod, merge_in, proj_w, None, merge=True, project=True)
        x_src = (x_all,)
```

```python
import functools
import math

import jax
import jax.numpy as jnp
from jax import lax
from jax.experimental import pallas as pl
from jax.experimental.pallas import tpu as pltpu

F32 = jnp.float32
BF16 = jnp.bfloat16

D_MODEL = 1024
BATCH = 16
SEQ = 2048
DEPTH = 4
GRID_W = 64
CTX_LEN = 256
LT = CTX_LEN + SEQ
S5_WIDTH = 256
S5_GROUPS = 16
S5_STATE = 64
S5_NSTATE = S5_GROUPS * S5_STATE
LRU_WIDTH = 256
LRU_CONV = 4
LRU_C = 8.0
MLA_HEADS = 8
MLA_NOPE = 64
MLA_QK = 96
MLA_V = 64
MLA_WIDTH = 512
MLA_Q_RANK = 256
MLA_KV_RANK = 128
ATTN_SCALE = MLA_QK ** -0.5
ROPE_AXIS = 16
ROPE_BASE = 10000.0
EPS = 1e-6

LANES = 128
HEAD_PAD = LANES
QK_PAD = MLA_HEADS * HEAD_PAD
IN_PAD = 2048
MOD_ROWS = 24
ROW_BLK = 256
ROW_PAR = 4
N_ROW_BLK = LT // ROW_BLK
TL = 128
NCT = CTX_LEN // TL
NLT = SEQ // TL
NCH = NCT + NLT
HALO = 16
CBW = 512
MXU_TILE = 256

_VMEM_LIMIT = 56 * 1024 * 1024


def _split_bf16(v):
    hi = v.astype(BF16)
    lo = (v - hi.astype(F32)).astype(BF16)
    return hi, lo


def _sigmoid(v):
    return 0.5 * jnp.tanh(0.5 * v) + 0.5


def _silu(v):
    return v * _sigmoid(v)


def _rms(v, g):
    ms = jnp.mean(v * v, axis=-1, keepdims=True)
    return v * lax.rsqrt(ms + EPS) * g


def _layer_spec(l, shape):
    zeros = (0,) * len(shape)
    return pl.BlockSpec((None,) + shape, lambda a, b: (l,) + zeros)


def _mod_body(c_ref, w_ref, b_ref, o_ref):
    c = c_ref[...]
    sc = c * jax.nn.sigmoid(c)
    s_hi, s_lo = _split_bf16(sc)
    w_hi, w_lo = _split_bf16(w_ref[0])
    acc = jnp.dot(s_hi, w_hi, preferred_element_type=F32)
    acc += jnp.dot(s_hi, w_lo, preferred_element_type=F32)
    acc += jnp.dot(s_lo, w_hi, preferred_element_type=F32)
    o_ref[0] = acc + b_ref[0]


def _mod_tables(c_pad, ada_w, ada_b):
    nj = 3 * D_MODEL // 1024
    return pl.pallas_call(
        _mod_body,
        grid=(DEPTH, nj),
        in_specs=[
            pl.BlockSpec((MOD_ROWS, D_MODEL), lambda l, j: (0, 0)),
            pl.BlockSpec((1, D_MODEL, 1024), lambda l, j: (l, 0, j)),
            pl.BlockSpec((1, 1, 1024), lambda l, j: (l, 0, j)),
        ],
        out_specs=pl.BlockSpec((1, MOD_ROWS, 1024), lambda l, j: (l, 0, j)),
        out_shape=jax.ShapeDtypeStruct((DEPTH, MOD_ROWS, 3 * D_MODEL), F32),
        compiler_params=pltpu.CompilerParams(vmem_limit_bytes=_VMEM_LIMIT),
        name="adaln_tables",
    )(c_pad, ada_w, ada_b.reshape(DEPTH, 1, 3 * D_MODEL))


PROJ_US5, PROJ_XLRU, PROJ_ZLRU, PROJ_CQ, PROJ_KVR = 0, 2, 3, 4, 5


def _row_body(*refs, layer, merge, project, final, split_src):
    it = iter(refs)
    ctx_ref = next(it) if split_src else None
    x_ref, mod_ref = next(it), next(it)
    if merge:
        yf_ref, yb_ref, uz_ref, zlru_ref, ymla_ref, zmla_ref, d_ref, gw_ref, gb_ref, wo_ref = (
            next(it) for _ in range(10))
    fg_ref = next(it) if final else None
    if project:
        g_ref, wi_ref = next(it), next(it)
    outs = list(it)

    b = pl.program_id(0)
    t = pl.program_id(1) + (1 if final else 0)
    is_ctx = t == 0
    rows = ROW_PAR * ROW_BLK

    def flat(ref):
        return ref[...].reshape(rows, ref.shape[-1])

    def mod_vec(layer_, p, lo):
        mod_row = jnp.where(is_ctx, BATCH, b * ROW_PAR + p)
        return mod_ref[layer_, pl.ds(mod_row, 1), lo:lo + D_MODEL]

    def per_sample(v, fn):
        return jnp.concatenate([fn(v[p * ROW_BLK:(p + 1) * ROW_BLK], p) for p in range(ROW_PAR)], axis=0)

    x = flat(x_ref)
    if split_src:
        x = jnp.where(is_ctx, flat(ctx_ref), x)

    if merge:
        ysc = flat(yf_ref).astype(F32) + flat(yb_ref).astype(F32)
        uz = flat(uz_ref).astype(F32)
        y = d_ref[...] * uz[:, 0:S5_WIDTH] + ysc[:, 0:S5_WIDTH]
        g = jnp.dot(y.astype(BF16), gw_ref[...], preferred_element_type=F32) + gb_ref[...]
        y_s5 = g[:, 0:S5_WIDTH] * _sigmoid(g[:, S5_WIDTH:])
        cat = jnp.concatenate([
            (y_s5 * _silu(uz[:, S5_WIDTH:])).astype(BF16),
            (ysc[:, S5_WIDTH:] * _silu(flat(zlru_ref).astype(F32))).astype(BF16),
            (flat(ymla_ref).astype(F32) * _silu(flat(zmla_ref).astype(F32))).astype(BF16)], axis=-1)
        o = jnp.dot(cat, wo_ref[...], preferred_element_type=F32)
        x = x + per_sample(o, lambda v, p: mod_vec(layer, p, 2 * D_MODEL) * v)
        out = _rms(x, fg_ref[...]) if final else x
        out_ref = outs.pop(0)
        out_ref[...] = out.reshape(out_ref.shape)

    if project:
        nxt = layer + 1 if merge else layer
        h = per_sample(_rms(x, g_ref[...]),
                       lambda v, p: v * (1.0 + mod_vec(nxt, p, D_MODEL)) + mod_vec(nxt, p, 0))
        proj_ref = outs.pop(0)
        proj_ref[...] = jnp.dot(h.astype(BF16), wi_ref[...],
                                preferred_element_type=F32).astype(BF16).reshape(proj_ref.shape)


def _row_stage(layer, x_src, mod, merge_in, proj_w, final_g, *, merge, project, final=False):
    split_src = len(x_src) == 2
    t_off = 1 if final else 0
    row_spec = lambda w, cb=0: pl.BlockSpec((ROW_PAR, ROW_BLK, w), lambda b, t: (b, t + t_off, cb))
    const = lambda s: pl.BlockSpec(s, lambda b, t: (0,) * len(s))
    args, in_specs = [], []
    if split_src:
        args += list(x_src)
        in_specs += [pl.BlockSpec((ROW_PAR, ROW_BLK, D_MODEL), lambda b, t: (b, 0, 0)),
                     pl.BlockSpec((ROW_PAR, ROW_BLK, D_MODEL), lambda b, t: (b, jnp.maximum(t - 1, 0), 0))]
    else:
        args += list(x_src)
        in_specs += [row_spec(D_MODEL)]
    args.append(mod)
    in_specs.append(const((DEPTH, MOD_ROWS, 3 * D_MODEL)))
    if merge:
        y_fwd, y_bwd, proj, ymla, s5_d, glu_w, glu_b, w_out = merge_in
        args += [y_fwd, y_bwd, proj, proj, ymla, proj, s5_d, glu_w, glu_b, w_out]
        in_specs += [
            row_spec(512), row_spec(512),
            row_spec(512, 0),
            row_spec(256, PROJ_ZLRU),
            pl.BlockSpec((ROW_PAR, ROW_BLK, MLA_WIDTH), lambda b, t: (b, t, 0)),
            row_spec(512, 3),
            _layer_spec(layer, (1, S5_WIDTH)),
            _layer_spec(layer, (S5_WIDTH, 2 * S5_WIDTH)),
            _layer_spec(layer, (1, 2 * S5_WIDTH)),
            _layer_spec(layer, (D_MODEL, D_MODEL)),
        ]
    if final:
        args.append(final_g)
        in_specs.append(const((1, D_MODEL)))
    if project:
        nxt = layer + 1 if merge else layer
        norm_g, w_in_pad = proj_w
        args += [norm_g, w_in_pad]
        in_specs += [_layer_spec(nxt, (1, D_MODEL)), _layer_spec(nxt, (D_MODEL, IN_PAD))]

    out_shape, out_specs, aliases = [], [], {}
    if merge:
        if final:
            out_shape.append(jax.ShapeDtypeStruct((BATCH, SEQ, D_MODEL), F32))
            out_specs.append(pl.BlockSpec((ROW_PAR, ROW_BLK, D_MODEL), lambda b, t: (b, t, 0)))
        else:
            out_shape.append(jax.ShapeDtypeStruct((BATCH, LT, D_MODEL), F32))
            out_specs.append(row_spec(D_MODEL))
            if not split_src:
                aliases = {0: 0}
    if project:
        out_shape.append(jax.ShapeDtypeStruct((BATCH, LT, IN_PAD), BF16))
        out_specs.append(row_spec(IN_PAD))
    return pl.pallas_call(
        functools.partial(_row_body, layer=layer, merge=merge, project=project, final=final,
                          split_src=split_src),
        grid=(BATCH // ROW_PAR, N_ROW_BLK - t_off),
        in_specs=in_specs,
        out_specs=out_specs,
        out_shape=out_shape,
        input_output_aliases=aliases,
        compiler_params=pltpu.CompilerParams(vmem_limit_bytes=_VMEM_LIMIT),
        name="row_stage",
    )(*args)


def _chunk_index(bwd, i):
    if not bwd:
        return i
    return jnp.where(i < NCT, NCT - 1 - i, 2 * NCT + NLT - 1 - i)


def _time_major(v):
    t = v.shape[1]
    return jnp.swapaxes(v, 0, 1).reshape(t * BATCH, v.shape[2])


def _batch_major(v):
    return jnp.swapaxes(v.reshape(v.shape[0] // BATCH, BATCH, v.shape[1]), 0, 1)


def _scan_body(us5_ref, xl_ref, xp_ref, xn_ref, bb_ref, cc_ref, ab_ref, cw_ref, cb_ref,
               wg_ref, bg_ref, sp_ref, ysc_ref,
               xs, hb, xpad, la, lb, hs5, hlru, *, bwd):
    i = pl.program_id(0)
    c = _chunk_index(bwd, i)
    rows = TL * BATCH
    half = rows // 2

    @pl.when(i == 0)
    def _():
        hs5[...] = jnp.zeros_like(hs5)
        hlru[...] = jnp.zeros_like(hlru)

    def rows_of(k):
        t = TL - 1 - k if bwd else k
        return slice(t * BATCH, (t + 1) * BATCH)

    def half_rows(k):
        first = (1 - k) if bwd else k
        return slice(first * half, (first + 1) * half)

    prev_ok = jnp.where((c != 0) & (c != NCT), 1.0, 0.0).astype(F32)
    next_ok = jnp.where((c != NCT - 1) & (c != NCH - 1), 1.0, 0.0).astype(F32)
    xpad[0:2 * BATCH, :] = _time_major(xp_ref[...])[(HALO - 2) * BATCH:, :].astype(F32) * prev_ok
    xpad[pl.ds(2 * BATCH, rows), :] = _time_major(xl_ref[...]).astype(F32)
    xpad[pl.ds((TL + 2) * BATCH, BATCH), :] = _time_major(xn_ref[...])[0:BATCH, :].astype(F32) * next_ok

    def gates(r0):
        rs = slice(r0, r0 + ROW_BLK)
        xc = cb_ref[...]
        for k in range(LRU_CONV):
            xc = xc + xpad[r0 + k * BATCH:r0 + k * BATCH + ROW_BLK, :] * cw_ref[k:k + 1, :]
        th = jnp.tanh(jnp.dot(xc.astype(BF16), wg_ref[...], preferred_element_type=F32) + bg_ref[...])
        sp4 = sp_ref[...]
        neg_log_a = sp4 * th[:, 0:LRU_WIDTH] + sp4
        a = jnp.exp2(neg_log_a * (-math.log2(math.e)))
        z = jnp.maximum(jnp.tanh(neg_log_a) * (a * a + 1.0), 0.0)
        mult = jnp.where(z > 0.0, z * lax.rsqrt(z), 0.0)
        la[rs, :] = a
        lb[rs, :] = (mult * xc) * (0.5 * th[:, LRU_WIDTH:] + 0.5)

    ub = _time_major(us5_ref[...])
    blk_w = 2 * CBW
    n_blk = S5_NSTATE // CBW
    n_tiles = 2 * S5_NSTATE // MXU_TILE
    half_steps = TL // 2

    def b_tile(k, n):
        hr, ns = half_rows(k), slice(n * MXU_TILE, (n + 1) * MXU_TILE)
        xs[hr, ns] = jnp.dot(ub[hr], bb_ref[:, ns], preferred_element_type=F32)

    def load_carry(cb):
        st = slice(cb * CBW, (cb + 1) * CBW)
        return (hs5[0, :, st], hs5[1, :, st]) + ((hlru[...],) if cb == 0 else ())

    def save_carry(cb, carry):
        st = slice(cb * CBW, (cb + 1) * CBW)
        hs5[0, :, st] = carry[0]
        hs5[1, :, st] = carry[1]
        if cb == 0:
            hlru[...] = carry[2]

    def scan_steps(cb, carry, k0, k1):
        st = slice(cb * CBW, (cb + 1) * CBW)
        re = slice(cb * blk_w, cb * blk_w + CBW)
        im = slice(cb * blk_w + CBW, (cb + 1) * blk_w)
        ar = jnp.broadcast_to(ab_ref[0:1, st], (BATCH, CBW))
        ai = jnp.broadcast_to(ab_ref[1:2, st], (BATCH, CBW))
        for k in range(k0, k1):
            rs = rows_of(k)
            hr, hi = carry[0], carry[1]
            nr = ar * hr - ai * hi + xs[rs, re]
            ni = ar * hi + ai * hr + xs[rs, im]
            hb[rs, re] = nr.astype(BF16)
            hb[rs, im] = ni.astype(BF16)
            if cb == 0:
                hl = la[rs, :] * carry[2] + lb[rs, :]
                lb[rs, :] = hl
                carry = (nr, ni, hl)
            else:
                carry = (nr, ni)
        return carry

    def c_map(k):
        return jnp.dot(hb[half_rows(k), :], cc_ref[...], preferred_element_type=F32)

    gate_blocks = list(range(rows // ROW_BLK))
    if bwd:
        gate_blocks.reverse()
    n_gate = len(gate_blocks)
    for n in range(n_tiles):
        b_tile(0, n)
        for g in gate_blocks[n * n_gate // n_tiles:(n + 1) * n_gate // n_tiles]:
            gates(g * ROW_BLK)
    tiles_per_blk = n_tiles // n_blk
    seg = half_steps // tiles_per_blk
    for cb in range(n_blk):
        carry = load_carry(cb)
        for n in range(tiles_per_blk):
            carry = scan_steps(cb, carry, n * seg, (n + 1) * seg)
            b_tile(1, cb * tiles_per_blk + n)
        save_carry(cb, carry)
    y_a = c_map(0)
    for cb in range(n_blk):
        save_carry(cb, scan_steps(cb, load_carry(cb), half_steps, TL))
    y_b = c_map(1)
    y = jnp.concatenate([y_b, y_a] if bwd else [y_a, y_b], axis=0)
    ysc_ref[:, :, 0:S5_WIDTH] = _batch_major(y.astype(ysc_ref.dtype))
    ysc_ref[:, :, S5_WIDTH:] = _batch_major(lb[...].astype(ysc_ref.dtype))


def _scan(l, d, proj, bb, cc, ab, conv_w, conv_b, wg, bg, sp):
    bwd = d == 1
    n_halo = LT // HALO
    chunk = lambda cb: (lambda i: (0, _chunk_index(bwd, i), cb))
    prev = lambda i: (0, jnp.maximum(_chunk_index(bwd, i) * (TL // HALO) - 1, 0), PROJ_XLRU)
    nxt = lambda i: (0, jnp.minimum((_chunk_index(bwd, i) + 1) * (TL // HALO), n_halo - 1), PROJ_XLRU)
    per_dir = lambda s: pl.BlockSpec((None, None) + s, lambda i: (l, d) + (0,) * len(s))
    per_layer = lambda s: pl.BlockSpec((None,) + s, lambda i: (l,) + (0,) * len(s))
    out_spec = pl.BlockSpec((BATCH, TL, 512), lambda i: (0, _chunk_index(bwd, i), 0))
    rows = TL * BATCH
    return pl.pallas_call(
        functools.partial(_scan_body, bwd=bwd),
        grid=(NCH,),
        in_specs=[
            pl.BlockSpec((BATCH, TL, S5_WIDTH), chunk(PROJ_US5)),
            pl.BlockSpec((BATCH, TL, LRU_WIDTH), chunk(PROJ_XLRU)),
            pl.BlockSpec((BATCH, HALO, LRU_WIDTH), prev),
            pl.BlockSpec((BATCH, HALO, LRU_WIDTH), nxt),
            per_dir((S5_WIDTH, 2 * S5_NSTATE)),
            per_dir((2 * S5_NSTATE, S5_WIDTH)),
            per_dir((2, S5_NSTATE)),
            per_layer((LRU_CONV, LRU_WIDTH)),
            per_layer((1, LRU_WIDTH)),
            per_dir((LRU_WIDTH, 2 * LRU_WIDTH)),
            per_dir((1, 2 * LRU_WIDTH)),
            per_dir((1, LRU_WIDTH)),
        ],
        out_specs=out_spec,
        out_shape=jax.ShapeDtypeStruct((BATCH, LT, 512), BF16),
        scratch_shapes=[
            pltpu.VMEM((rows, 2 * S5_NSTATE), F32),
            pltpu.VMEM((rows, 2 * S5_NSTATE), BF16),
            pltpu.VMEM(((TL + 3) * BATCH, LRU_WIDTH), F32),
            pltpu.VMEM((rows, LRU_WIDTH), F32),
            pltpu.VMEM((rows, LRU_WIDTH), F32),
            pltpu.VMEM((2, BATCH, S5_NSTATE), F32),
            pltpu.VMEM((BATCH, LRU_WIDTH), F32),
        ],
        compiler_params=pltpu.CompilerParams(vmem_limit_bytes=_VMEM_LIMIT),
        name="s5_lru_scan",
    )(proj, proj, proj, proj, bb, cc, ab, conv_w, conv_b, wg, bg, sp)


def _rope(v, cos, sin):
    width = v.shape[-1]
    lane = lax.broadcasted_iota(jnp.int32, v.shape, 1)
    first_half = ((lane // 8) % 2) == 0
    swapped = jnp.where(first_half, pltpu.roll(v, width - 8, 1), pltpu.roll(v, 8, 1))
    return v * cos + swapped * sin


def _attn_body(cq_ref, kvr_ref, cosq_ref, sinq_ref, cosk_ref, sink_ref, qn_ref, wuq_ref,
               kvn_ref, wukv_ref, o_ref, kt_s, v_s, *, q_off):
    j = pl.program_id(1)

    @pl.when(j == 0)
    def _prep():
        lane = lax.broadcasted_iota(jnp.int32, (ROW_BLK, QK_PAD), 1)
        ones_hi = jnp.where(lane % HEAD_PAD >= MLA_V, 1.0, 0.0).astype(F32)

        def blk(r, carry):
            rs = pl.ds(pl.multiple_of(r * ROW_BLK, ROW_BLK), ROW_BLK)
            kvr = kvr_ref[0, rs, :].astype(F32)
            kvn = _rms(kvr[:, 0:MLA_KV_RANK], kvn_ref[...])
            kv = jnp.dot(kvn.astype(BF16), wukv_ref[...], preferred_element_type=F32)
            kro = _rope(kvr[:, MLA_KV_RANK:], cosk_ref[rs, :], sink_ref[rs, :])
            for h in range(MLA_HEADS):
                hs = slice(h * HEAD_PAD, (h + 1) * HEAD_PAD)
                kt_s[r, hs, :] = (kv[:, hs] + kro).T.astype(BF16)
            v_s[rs, :] = (kv[:, QK_PAD:] + ones_hi).astype(BF16)
            return carry
        lax.fori_loop(0, N_ROW_BLK, blk, 0)

    qn = _rms(cq_ref[0].astype(F32), qn_ref[...])
    q = jnp.dot(qn.astype(BF16), wuq_ref[...], preferred_element_type=F32)
    cosq = cosq_ref[...]
    sinq = sinq_ref[...]
    lane = lax.broadcasted_iota(jnp.int32, (ROW_BLK, HEAD_PAD), 1)

    def head_slice(h):
        return slice(h * HEAD_PAD, (h + 1) * HEAD_PAD)

    def roped_q(h):
        return _rope(q[:, head_slice(h)], cosq, sinq).astype(BF16)

    def row_max(blocks):
        mx = blocks[0]
        for sb in blocks[1:]:
            mx = jnp.maximum(mx, sb)
        return jnp.max(mx, axis=-1, keepdims=True)

    def attend(n_blk):
        score = lambda qh, h, r: jnp.dot(qh, kt_s[r, head_slice(h), :], preferred_element_type=F32)
        q_cur = roped_q(0)
        s_cur = [score(q_cur, 0, r) for r in range(n_blk)]
        m_cur = row_max(s_cur)
        pv = []
        for h in range(MLA_HEADS):
            more = h + 1 < MLA_HEADS
            if more:
                q_nxt = roped_q(h + 1)
            p_blocks, s_nxt = [], []
            for r in range(n_blk):
                p_blocks.append(jnp.exp2(s_cur[r] - m_cur).astype(BF16))
                if more:
                    s_nxt.append(score(q_nxt, h + 1, r))
            p = jnp.concatenate(p_blocks, axis=-1)
            pv.append(jnp.dot(p, v_s[0:n_blk * ROW_BLK, head_slice(h)], preferred_element_type=F32))
            if more:
                s_cur, m_cur = s_nxt, row_max(s_nxt)
            if h % 2 == 1:
                o_even = pv[h - 1] / pltpu.roll(pv[h - 1], MLA_V, 1)
                o_odd = pltpu.roll(pv[h], MLA_V, 1) / pv[h]
                o_ref[0, :, (h // 2) * LANES:(h // 2 + 1) * LANES] = jnp.where(
                    lane < MLA_V, o_even, o_odd).astype(o_ref.dtype)

    is_ctx = (j + q_off) == 0

    @pl.when(is_ctx)
    def _():
        attend(CTX_LEN // ROW_BLK)

    @pl.when(jnp.logical_not(is_ctx))
    def _():
        attend(N_ROW_BLK)


def _attention(l, proj, cosq, sinq, cosk, sink, q_norm, w_uq_pad, kv_norm, w_ukv_pad, with_ctx):
    q_off = 0 if with_ctx else 1
    const = lambda s: pl.BlockSpec(s, lambda b, j: (0,) * len(s))
    return pl.pallas_call(
        functools.partial(_attn_body, q_off=q_off),
        grid=(BATCH, N_ROW_BLK - q_off),
        in_specs=[
            pl.BlockSpec((1, ROW_BLK, MLA_Q_RANK), lambda b, j: (b, j + q_off, PROJ_CQ)),
            pl.BlockSpec((1, LT, 256), lambda b, j: (b, 0, PROJ_KVR)),
            pl.BlockSpec((ROW_BLK, HEAD_PAD), lambda b, j: (j + q_off, 0)),
            pl.BlockSpec((ROW_BLK, HEAD_PAD), lambda b, j: (j + q_off, 0)),
            const((LT, HEAD_PAD)),
            const((LT, HEAD_PAD)),
            _layer_spec(l, (1, MLA_Q_RANK)),
            _layer_spec(l, (MLA_Q_RANK, QK_PAD)),
            _layer_spec(l, (1, MLA_KV_RANK)),
            _layer_spec(l, (MLA_KV_RANK, 2 * QK_PAD)),
        ],
        out_specs=pl.BlockSpec((1, ROW_BLK, MLA_WIDTH), lambda b, j: (b, j, 0)),
        out_shape=jax.ShapeDtypeStruct((BATCH, LT - q_off * CTX_LEN, MLA_WIDTH), BF16),
        scratch_shapes=[
            pltpu.VMEM((N_ROW_BLK, QK_PAD, ROW_BLK), BF16),
            pltpu.VMEM((LT, QK_PAD), BF16),
        ],
        compiler_params=pltpu.CompilerParams(vmem_limit_bytes=_VMEM_LIMIT),
        name="mla_attention",
    )(proj, proj, cosq, sinq, cosk, sink, q_norm, w_uq_pad, kv_norm, w_ukv_pad)


def _block_diag(blocks):
    n, r, c = blocks.shape
    tiled = jnp.tile(blocks.reshape(n * r, c), (1, n))
    on_diag = (jnp.arange(n * r)[:, None] // r) == (jnp.arange(n * c)[None, :] // c)
    return jnp.where(on_diag, tiled, jnp.zeros_like(tiled))


def _s5_matrices(lam_re, lam_im, log_dt, b_re, b_im, c_re, c_im):
    dt = jnp.exp(log_dt)[:, None]
    mag = jnp.exp(lam_re * dt)
    ab_re, ab_im = mag * jnp.cos(lam_im * dt), mag * jnp.sin(lam_im * dt)
    den = lam_re * lam_re + lam_im * lam_im
    nr = ab_re - 1.0
    coef_re = (nr * lam_re + ab_im * lam_im) / den
    coef_im = (ab_im * lam_re - nr * lam_im) / den
    bb_re = coef_re[..., None] * b_re - coef_im[..., None] * b_im
    bb_im = coef_re[..., None] * b_im + coef_im[..., None] * b_re
    to_in = lambda m: _block_diag(jnp.swapaxes(m, 1, 2))
    to_out = lambda m: _block_diag(jnp.swapaxes(m, 1, 2))

    def grouped(re, im, axis):
        parts = []
        for cb in range(S5_NSTATE // CBW):
            parts += [lax.slice_in_dim(re, cb * CBW, (cb + 1) * CBW, axis=axis),
                      lax.slice_in_dim(im, cb * CBW, (cb + 1) * CBW, axis=axis)]
        return jnp.concatenate(parts, axis=axis)

    bb = grouped(to_in(bb_re), to_in(bb_im), 1)
    cc = grouped(to_out(c_re), -to_out(c_im), 0)
    ab = jnp.stack([ab_re.reshape(-1), ab_im.reshape(-1)])
    return bb.astype(BF16), cc.astype(BF16), ab


def _rope_tables():
    n_rows = SEQ // GRID_W
    rows = jnp.repeat(jnp.arange(n_rows, dtype=F32), GRID_W)
    cols = jnp.tile(jnp.arange(GRID_W, dtype=F32), n_rows)
    inv_freq = ROPE_BASE ** (-jnp.arange(0, ROPE_AXIS, 2, dtype=F32) / ROPE_AXIS)
    ang_r = rows[:, None] * inv_freq
    ang_c = cols[:, None] * inv_freq
    cr, sr, cc, sc = jnp.cos(ang_r), jnp.sin(ang_r), jnp.cos(ang_c), jnp.sin(ang_c)
    cos32 = jnp.concatenate([cr, cr, cc, cc], axis=-1)
    sin32 = jnp.concatenate([-sr, sr, -sc, sc], axis=-1)
    cos = jnp.ones((LT, HEAD_PAD), F32).at[CTX_LEN:, MLA_NOPE:MLA_QK].set(cos32)
    sin = jnp.zeros((LT, HEAD_PAD), F32).at[CTX_LEN:, MLA_NOPE:MLA_QK].set(sin32)
    return cos, sin


def _pad_heads(w, used):
    w = jnp.pad(w, ((0, 0), (0, 0), (0, 0), (0, HEAD_PAD - used)))
    return w.reshape(w.shape[0], w.shape[1], MLA_HEADS * HEAD_PAD)


def kernel(x, c, ctx, c_ctx, ada_w, ada_b, norm_g, w_in, s5_lam_re, s5_lam_im, s5_log_dt, s5_b_re,
           s5_b_im, s5_c_re, s5_c_im, s5_d, s5_glu_w, s5_glu_b, lru_conv_w, lru_conv_b, lru_wa,
           lru_ba, lru_wx, lru_bx, lru_lam, mla_q_norm, mla_w_uq, mla_kv_norm, mla_w_ukv, w_out,
           final_g):
    c_pad = jnp.zeros((MOD_ROWS, D_MODEL), F32).at[:BATCH].set(c).at[BATCH].set(c_ctx)
    mod = _mod_tables(c_pad, ada_w, ada_b)

    cosk, sink = _rope_tables()
    q_mul = ATTN_SCALE * math.log2(math.e)
    cosq, sinq = cosk * q_mul, sink * q_mul

    kr_pad = jnp.zeros((DEPTH, D_MODEL, HEAD_PAD), F32).at[:, :, MLA_NOPE:MLA_QK].set(w_in[:, :, 1408:1440])
    w_in_pad = jnp.concatenate([w_in[:, :, 0:1408], kr_pad, w_in[:, :, 1440:1952]], axis=2).astype(BF16)
    bb, cc, ab = jax.vmap(jax.vmap(_s5_matrices))(s5_lam_re, s5_lam_im, s5_log_dt, s5_b_re, s5_b_im,
                                                  s5_c_re, s5_c_im)
    bd = jax.vmap(jax.vmap(_block_diag))
    wg = (0.5 * jnp.concatenate([bd(lru_wa), bd(lru_wx)], axis=-1)).astype(BF16)
    bg = 0.5 * jnp.concatenate([lru_ba, lru_bx], axis=-1)[:, :, None, :]
    sp = (0.5 * LRU_C) * jax.nn.softplus(-lru_lam)[:, :, None, :]
    w_uq_pad = _pad_heads(mla_w_uq.reshape(DEPTH, MLA_Q_RANK, MLA_HEADS, MLA_QK), MLA_QK).astype(BF16)
    ukv = mla_w_ukv.reshape(DEPTH, MLA_KV_RANK, MLA_HEADS, MLA_NOPE + MLA_V)
    w_ukv_pad = jnp.concatenate([_pad_heads(ukv[..., :MLA_NOPE], MLA_NOPE),
                                 _pad_heads(ukv[..., MLA_NOPE:], MLA_V)], axis=-1).astype(BF16)
    glu_w = s5_glu_w.astype(BF16)
    w_out_b = w_out.astype(BF16)
    row = lambda a: a[:, None, :]

    proj_w = (row(norm_g), w_in_pad)
    x_src = (ctx, x)
    (proj,) = _row_stage(0, x_src, mod, None, proj_w, None, merge=False, project=True)
    for l in range(DEPTH):
        final = l == DEPTH - 1
        y_fwd, y_bwd = (_scan(l, d, proj, bb, cc, ab, lru_conv_w, row(lru_conv_b), wg, bg, sp)
                        for d in range(2))
        ymla = _attention(l, proj, cosq, sinq, cosk, sink, row(mla_q_norm), w_uq_pad,
                          row(mla_kv_norm), w_ukv_pad, with_ctx=not final)
        merge_in = (y_fwd, y_bwd, proj, ymla, row(s5_d), glu_w, row(s5_glu_b), w_out_b)
        if final:
            (out,) = _row_stage(l, x_src, mod, merge_in, None, final_g[None], merge=True,
                                project=False, final=True)
            return out
        x_all, proj = _row_stage(l, x_src, mod, merge_in, proj_w, None, merge=True, project=True)
        x_src = (x_all,)
```

```python
import functools
import math

import jax
import jax.numpy as jnp
from jax import lax
from jax.experimental import pallas as pl
from jax.experimental.pallas import tpu as pltpu

F32 = jnp.float32
BF16 = jnp.bfloat16

D_MODEL = 1024
BATCH = 16
SEQ = 2048
DEPTH = 4
GRID_W = 64
CTX_LEN = 256
LT = CTX_LEN + SEQ
S5_WIDTH = 256
S5_GROUPS = 16
S5_STATE = 64
S5_NSTATE = S5_GROUPS * S5_STATE
LRU_WIDTH = 256
LRU_CONV = 4
LRU_C = 8.0
MLA_HEADS = 8
MLA_NOPE = 64
MLA_QK = 96
MLA_V = 64
MLA_WIDTH = 512
MLA_Q_RANK = 256
MLA_KV_RANK = 128
ATTN_SCALE = MLA_QK ** -0.5
ROPE_AXIS = 16
ROPE_BASE = 10000.0
EPS = 1e-6

LANES = 128
HEAD_PAD = LANES
QK_PAD = MLA_HEADS * HEAD_PAD
IN_PAD = 2048
KR_LO, KR_HI, D_IN = 1408, 1440, 1952
MOD_ROWS = 24
ROW_BLK = 256
ROW_PAR = 4
N_ROW_BLK = LT // ROW_BLK
TL = 128
NCT = CTX_LEN // TL
NLT = SEQ // TL
NCH = NCT + NLT
HALO = 16
CBW = 512
MXU_TILE = 256
Q_BLK = 512
Q_MUL = ATTN_SCALE * math.log2(math.e)

_VMEM_LIMIT = 56 * 1024 * 1024


def _split_bf16(v):
    hi = v.astype(BF16)
    lo = (v - hi.astype(F32)).astype(BF16)
    return hi, lo


def _sigmoid(v):
    return 0.5 * jnp.tanh(0.5 * v) + 0.5


def _silu(v):
    return v * _sigmoid(v)


def _rms(v, g):
    ms = jnp.mean(v * v, axis=-1, keepdims=True)
    return v * lax.rsqrt(ms + EPS) * g


def _layer_spec(l, shape):
    zeros = (0,) * len(shape)
    return pl.BlockSpec((None,) + shape, lambda a, b: (l,) + zeros)


def _mod_body(c_ref, w_ref, b_ref, o_ref):
    c = c_ref[...]
    sc = c * jax.nn.sigmoid(c)
    s_hi, s_lo = _split_bf16(sc)
    w_hi, w_lo = _split_bf16(w_ref[0])
    acc = jnp.dot(s_hi, w_hi, preferred_element_type=F32)
    acc += jnp.dot(s_hi, w_lo, preferred_element_type=F32)
    acc += jnp.dot(s_lo, w_hi, preferred_element_type=F32)
    o_ref[0] = acc + b_ref[0]


def _mod_tables(c_pad, ada_w, ada_b):
    nj = 3 * D_MODEL // 1024
    return pl.pallas_call(
        _mod_body,
        grid=(DEPTH, nj),
        in_specs=[
            pl.BlockSpec((MOD_ROWS, D_MODEL), lambda l, j: (0, 0)),
            pl.BlockSpec((1, D_MODEL, 1024), lambda l, j: (l, 0, j)),
            pl.BlockSpec((1, 1, 1024), lambda l, j: (l, 0, j)),
        ],
        out_specs=pl.BlockSpec((1, MOD_ROWS, 1024), lambda l, j: (l, 0, j)),
        out_shape=jax.ShapeDtypeStruct((DEPTH, MOD_ROWS, 3 * D_MODEL), F32),
        compiler_params=pltpu.CompilerParams(vmem_limit_bytes=_VMEM_LIMIT),
        name="adaln_tables",
    )(c_pad, ada_w, ada_b.reshape(DEPTH, 1, 3 * D_MODEL))


PROJ_US5, PROJ_XLRU, PROJ_ZLRU, PROJ_CQ, PROJ_KVR = 0, 2, 3, 4, 5


def _row_body(*refs, layer, merge, project, final, split_src):
    it = iter(refs)
    ctx_ref = next(it) if split_src else None
    x_ref, mod_ref = next(it), next(it)
    if merge:
        yf_ref, yb_ref, uz_ref, zlru_ref = (next(it) for _ in range(4))
        ymc_ref = None if final else next(it)
        yml_ref, zmla_ref, d_ref, gw_ref, gb_ref, wo_ref = (next(it) for _ in range(6))
    fg_ref = next(it) if final else None
    if project:
        g_ref, wi_ref = next(it), next(it)
    outs = list(it)

    b = pl.program_id(0)
    t = pl.program_id(1) + (1 if final else 0)
    is_ctx = t == 0
    rows = ROW_PAR * ROW_BLK

    def flat(ref):
        return ref[...].reshape(rows, ref.shape[-1])

    def mod_vec(layer_, p, lo):
        mod_row = jnp.where(is_ctx, BATCH, b * ROW_PAR + p)
        return mod_ref[layer_, pl.ds(mod_row, 1), lo:lo + D_MODEL]

    def per_sample(v, fn):
        return jnp.concatenate([fn(v[p * ROW_BLK:(p + 1) * ROW_BLK], p) for p in range(ROW_PAR)], axis=0)

    x = flat(x_ref)
    if split_src:
        x = jnp.where(is_ctx, flat(ctx_ref), x)

    if merge:
        ysc = flat(yf_ref).astype(F32) + flat(yb_ref).astype(F32)
        uz = flat(uz_ref).astype(F32)
        y = d_ref[...] * uz[:, 0:S5_WIDTH] + ysc[:, 0:S5_WIDTH]
        g = jnp.dot(y.astype(BF16), gw_ref[...], preferred_element_type=F32) + gb_ref[...]
        y_s5 = g[:, 0:S5_WIDTH] * _sigmoid(g[:, S5_WIDTH:])
        ymla = flat(yml_ref) if final else jnp.where(is_ctx, flat(ymc_ref), flat(yml_ref))
        cat = jnp.concatenate([
            (y_s5 * _silu(uz[:, S5_WIDTH:])).astype(BF16),
            (ysc[:, S5_WIDTH:] * _silu(flat(zlru_ref).astype(F32))).astype(BF16),
            (ymla.astype(F32) * _silu(flat(zmla_ref).astype(F32))).astype(BF16)], axis=-1)
        o = jnp.dot(cat, wo_ref[...], preferred_element_type=F32)
        x = x + per_sample(o, lambda v, p: mod_vec(layer, p, 2 * D_MODEL) * v)
        out = _rms(x, fg_ref[...]) if final else x
        out_ref = outs.pop(0)
        out_ref[...] = out.reshape(out_ref.shape)

    if project:
        nxt = layer + 1 if merge else layer
        h = per_sample(_rms(x, g_ref[...]),
                       lambda v, p: v * (1.0 + mod_vec(nxt, p, D_MODEL)) + mod_vec(nxt, p, 0))
        proj_ref = outs.pop(0)
        proj_ref[...] = jnp.dot(h.astype(BF16), wi_ref[...],
                                preferred_element_type=F32).astype(BF16).reshape(proj_ref.shape)


def _row_stage(layer, x_src, mod, merge_in, proj_w, final_g, *, merge, project, final=False):
    split_src = len(x_src) == 2
    t_off = 1 if final else 0
    row_spec = lambda w, cb=0: pl.BlockSpec((ROW_PAR, ROW_BLK, w), lambda b, t: (b, t + t_off, cb))
    const = lambda s: pl.BlockSpec(s, lambda b, t: (0,) * len(s))
    args, in_specs = [], []
    if split_src:
        args += list(x_src)
        in_specs += [pl.BlockSpec((ROW_PAR, ROW_BLK, D_MODEL), lambda b, t: (b, 0, 0)),
                     pl.BlockSpec((ROW_PAR, ROW_BLK, D_MODEL), lambda b, t: (b, jnp.maximum(t - 1, 0), 0))]
    else:
        args += list(x_src)
        in_specs += [row_spec(D_MODEL)]
    args.append(mod)
    in_specs.append(const((DEPTH, MOD_ROWS, 3 * D_MODEL)))
    if merge:
        y_fwd, y_bwd, proj, ymla_ctx, ymla_lat, s5_d, glu_w, glu_b, w_out = merge_in
        mla_args = [ymla_lat] if final else [ymla_ctx, ymla_lat]
        mla_block = (ROW_PAR, ROW_BLK, MLA_WIDTH)
        mla_specs = [pl.BlockSpec(mla_block, lambda b, t: (b, t, 0))] if final else [
            pl.BlockSpec(mla_block, lambda b, t: (b, 0, 0)),
            pl.BlockSpec(mla_block, lambda b, t: (b, jnp.maximum(t - 1, 0), 0))]
        args += [y_fwd, y_bwd, proj, proj, *mla_args, proj, s5_d, glu_w, glu_b, w_out]
        in_specs += [
            row_spec(512), row_spec(512),
            row_spec(512, 0),
            row_spec(256, PROJ_ZLRU),
            *mla_specs,
            row_spec(512, 3),
            _layer_spec(layer, (1, S5_WIDTH)),
            _layer_spec(layer, (S5_WIDTH, 2 * S5_WIDTH)),
            _layer_spec(layer, (1, 2 * S5_WIDTH)),
            _layer_spec(layer, (D_MODEL, D_MODEL)),
        ]
    if final:
        args.append(final_g)
        in_specs.append(const((1, D_MODEL)))
    if project:
        nxt = layer + 1 if merge else layer
        norm_g, w_in_pad = proj_w
        args += [norm_g, w_in_pad]
        in_specs += [_layer_spec(nxt, (1, D_MODEL)), _layer_spec(nxt, (D_MODEL, IN_PAD))]

    out_shape, out_specs, aliases = [], [], {}
    if merge:
        if final:
            out_shape.append(jax.ShapeDtypeStruct((BATCH, SEQ, D_MODEL), F32))
            out_specs.append(pl.BlockSpec((ROW_PAR, ROW_BLK, D_MODEL), lambda b, t: (b, t, 0)))
        else:
            out_shape.append(jax.ShapeDtypeStruct((BATCH, LT, D_MODEL), F32))
            out_specs.append(row_spec(D_MODEL))
            if not split_src:
                aliases = {0: 0}
    if project:
        out_shape.append(jax.ShapeDtypeStruct((BATCH, LT, IN_PAD), BF16))
        out_specs.append(row_spec(IN_PAD))
    return pl.pallas_call(
        functools.partial(_row_body, layer=layer, merge=merge, project=project, final=final,
                          split_src=split_src),
        grid=(BATCH // ROW_PAR, N_ROW_BLK - t_off),
        in_specs=in_specs,
        out_specs=out_specs,
        out_shape=out_shape,
        input_output_aliases=aliases,
        compiler_params=pltpu.CompilerParams(vmem_limit_bytes=_VMEM_LIMIT),
        name="row_stage",
    )(*args)


def _chunk_index(bwd, i):
    if not bwd:
        return i
    return jnp.where(i < NCT, NCT - 1 - i, 2 * NCT + NLT - 1 - i)


def _time_major(v):
    t = v.shape[1]
    return jnp.swapaxes(v, 0, 1).reshape(t * BATCH, v.shape[2])


def _batch_major(v):
    return jnp.swapaxes(v.reshape(v.shape[0] // BATCH, BATCH, v.shape[1]), 0, 1)


def _scan_body(us5_ref, xl_ref, xp_ref, xn_ref, bb_ref, cc_ref, ab_ref, cw_ref, cb_ref,
               wg_ref, bg_ref, sp_ref, ysc_ref,
               xs, hb, xpad, la, lb, hs5, hlru, *, bwd):
    i = pl.program_id(0)
    c = _chunk_index(bwd, i)
    rows = TL * BATCH
    half = rows // 2

    @pl.when(i == 0)
    def _():
        hs5[...] = jnp.zeros_like(hs5)
        hlru[...] = jnp.zeros_like(hlru)

    def rows_of(k):
        t = TL - 1 - k if bwd else k
        return slice(t * BATCH, (t + 1) * BATCH)

    def half_rows(k):
        first = (1 - k) if bwd else k
        return slice(first * half, (first + 1) * half)

    prev_ok = jnp.where((c != 0) & (c != NCT), 1.0, 0.0).astype(F32)
    next_ok = jnp.where((c != NCT - 1) & (c != NCH - 1), 1.0, 0.0).astype(F32)
    xpad[0:2 * BATCH, :] = _time_major(xp_ref[...])[(HALO - 2) * BATCH:, :].astype(F32) * prev_ok
    xpad[pl.ds(2 * BATCH, rows), :] = _time_major(xl_ref[...]).astype(F32)
    xpad[pl.ds((TL + 2) * BATCH, BATCH), :] = _time_major(xn_ref[...])[0:BATCH, :].astype(F32) * next_ok

    def gates(r0):
        rs = slice(r0, r0 + ROW_BLK)
        xc = cb_ref[...]
        for k in range(LRU_CONV):
            xc = xc + xpad[r0 + k * BATCH:r0 + k * BATCH + ROW_BLK, :] * cw_ref[k:k + 1, :]
        th = jnp.tanh(jnp.dot(xc.astype(BF16), wg_ref[...], preferred_element_type=F32) + bg_ref[...])
        sp4 = sp_ref[...]
        neg_log_a = sp4 * th[:, 0:LRU_WIDTH] + sp4
        a = jnp.exp2(neg_log_a * (-math.log2(math.e)))
        z = jnp.maximum(jnp.tanh(neg_log_a) * (a * a + 1.0), 0.0)
        mult = jnp.where(z > 0.0, z * lax.rsqrt(z), 0.0)
        la[rs, :] = a
        lb[rs, :] = (mult * xc) * (0.5 * th[:, LRU_WIDTH:] + 0.5)

    ub = _time_major(us5_ref[...])
    blk_w = 2 * CBW
    n_blk = S5_NSTATE // CBW
    n_tiles = 2 * S5_NSTATE // MXU_TILE
    half_steps = TL // 2

    def b_tile(k, n):
        hr, ns = half_rows(k), slice(n * MXU_TILE, (n + 1) * MXU_TILE)
        xs[hr, ns] = jnp.dot(ub[hr], bb_ref[:, ns], preferred_element_type=F32)

    def load_carry(cb):
        st = slice(cb * CBW, (cb + 1) * CBW)
        return (hs5[0, :, st], hs5[1, :, st]) + ((hlru[...],) if cb == 0 else ())

    def save_carry(cb, carry):
        st = slice(cb * CBW, (cb + 1) * CBW)
        hs5[0, :, st] = carry[0]
        hs5[1, :, st] = carry[1]
        if cb == 0:
            hlru[...] = carry[2]

    def scan_steps(cb, carry, k0, k1):
        st = slice(cb * CBW, (cb + 1) * CBW)
        re = slice(cb * blk_w, cb * blk_w + CBW)
        im = slice(cb * blk_w + CBW, (cb + 1) * blk_w)
        ar = jnp.broadcast_to(ab_ref[0:1, st], (BATCH, CBW))
        ai = jnp.broadcast_to(ab_ref[1:2, st], (BATCH, CBW))
        for k in range(k0, k1):
            rs = rows_of(k)
            hr, hi = carry[0], carry[1]
            nr = ar * hr - ai * hi + xs[rs, re]
            ni = ar * hi + ai * hr + xs[rs, im]
            hb[rs, re] = nr.astype(BF16)
            hb[rs, im] = ni.astype(BF16)
            if cb == 0:
                hl = la[rs, :] * carry[2] + lb[rs, :]
                lb[rs, :] = hl
                carry = (nr, ni, hl)
            else:
                carry = (nr, ni)
        return carry

    def c_map(k):
        return jnp.dot(hb[half_rows(k), :], cc_ref[...], preferred_element_type=F32)

    gate_blocks = list(range(rows // ROW_BLK))
    if bwd:
        gate_blocks.reverse()
    n_gate = len(gate_blocks)
    for n in range(n_tiles):
        b_tile(0, n)
        for g in gate_blocks[n * n_gate // n_tiles:(n + 1) * n_gate // n_tiles]:
            gates(g * ROW_BLK)
    tiles_per_blk = n_tiles // n_blk
    seg = half_steps // tiles_per_blk
    for cb in range(n_blk):
        carry = load_carry(cb)
        for n in range(tiles_per_blk):
            carry = scan_steps(cb, carry, n * seg, (n + 1) * seg)
            b_tile(1, cb * tiles_per_blk + n)
        save_carry(cb, carry)
    y_a = c_map(0)
    for cb in range(n_blk):
        save_carry(cb, scan_steps(cb, load_carry(cb), half_steps, TL))
    y_b = c_map(1)
    y = jnp.concatenate([y_b, y_a] if bwd else [y_a, y_b], axis=0)
    ysc_ref[:, :, 0:S5_WIDTH] = _batch_major(y.astype(ysc_ref.dtype))
    ysc_ref[:, :, S5_WIDTH:] = _batch_major(lb[...].astype(ysc_ref.dtype))


def _scan(l, d, proj, bb, cc, ab, conv_w, conv_b, wg, bg, sp):
    bwd = d == 1
    n_halo = LT // HALO
    chunk = lambda cb: (lambda i: (0, _chunk_index(bwd, i), cb))
    prev = lambda i: (0, jnp.maximum(_chunk_index(bwd, i) * (TL // HALO) - 1, 0), PROJ_XLRU)
    nxt = lambda i: (0, jnp.minimum((_chunk_index(bwd, i) + 1) * (TL // HALO), n_halo - 1), PROJ_XLRU)
    per_dir = lambda s: pl.BlockSpec((None, None) + s, lambda i: (l, d) + (0,) * len(s))
    per_layer = lambda s: pl.BlockSpec((None,) + s, lambda i: (l,) + (0,) * len(s))
    out_spec = pl.BlockSpec((BATCH, TL, 512), lambda i: (0, _chunk_index(bwd, i), 0))
    rows = TL * BATCH
    return pl.pallas_call(
        functools.partial(_scan_body, bwd=bwd),
        grid=(NCH,),
        in_specs=[
            pl.BlockSpec((BATCH, TL, S5_WIDTH), chunk(PROJ_US5)),
            pl.BlockSpec((BATCH, TL, LRU_WIDTH), chunk(PROJ_XLRU)),
            pl.BlockSpec((BATCH, HALO, LRU_WIDTH), prev),
            pl.BlockSpec((BATCH, HALO, LRU_WIDTH), nxt),
            per_dir((S5_WIDTH, 2 * S5_NSTATE)),
            per_dir((2 * S5_NSTATE, S5_WIDTH)),
            per_dir((2, S5_NSTATE)),
            per_layer((LRU_CONV, LRU_WIDTH)),
            per_layer((1, LRU_WIDTH)),
            per_dir((LRU_WIDTH, 2 * LRU_WIDTH)),
            per_dir((1, 2 * LRU_WIDTH)),
            per_dir((1, LRU_WIDTH)),
        ],
        out_specs=out_spec,
        out_shape=jax.ShapeDtypeStruct((BATCH, LT, 512), BF16),
        scratch_shapes=[
            pltpu.VMEM((rows, 2 * S5_NSTATE), F32),
            pltpu.VMEM((rows, 2 * S5_NSTATE), BF16),
            pltpu.VMEM(((TL + 3) * BATCH, LRU_WIDTH), F32),
            pltpu.VMEM((rows, LRU_WIDTH), F32),
            pltpu.VMEM((rows, LRU_WIDTH), F32),
            pltpu.VMEM((2, BATCH, S5_NSTATE), F32),
            pltpu.VMEM((BATCH, LRU_WIDTH), F32),
        ],
        compiler_params=pltpu.CompilerParams(vmem_limit_bytes=_VMEM_LIMIT),
        name="s5_lru_scan",
    )(proj, proj, proj, proj, bb, cc, ab, conv_w, conv_b, wg, bg, sp)


def _rope(v, cos, sin):
    width = v.shape[-1]
    lane = lax.broadcasted_iota(jnp.int32, v.shape, 1)
    first_half = ((lane // 8) % 2) == 0
    swapped = jnp.where(first_half, pltpu.roll(v, width - 8, 1), pltpu.roll(v, 8, 1))
    return v * cos + swapped * sin


def _attn_body(*refs, with_ctx):
    it = iter(refs)
    cqc_ref = next(it) if with_ctx else None
    (cql_ref, kvr_ref, cosq_ref, sinq_ref, cosk_ref, sink_ref, qn_ref, wuq_ref, kvn_ref,
     wukv_ref) = (next(it) for _ in range(10))
    oc_ref = next(it) if with_ctx else None
    ol_ref, kt_s, v_s = next(it), next(it), next(it)
    j = pl.program_id(1)

    @pl.when(j == 0)
    def _prep():
        lane = lax.broadcasted_iota(jnp.int32, (ROW_BLK, QK_PAD), 1)
        ones_hi = jnp.where(lane % HEAD_PAD >= MLA_V, 1.0, 0.0).astype(F32)

        def blk(r, carry):
            rs = pl.ds(pl.multiple_of(r * ROW_BLK, ROW_BLK), ROW_BLK)
            kvr = kvr_ref[0, rs, :].astype(F32)
            kvn = _rms(kvr[:, 0:MLA_KV_RANK], kvn_ref[...])
            kv = jnp.dot(kvn.astype(BF16), wukv_ref[...], preferred_element_type=F32)
            kro = _rope(kvr[:, MLA_KV_RANK:], cosk_ref[rs, :], sink_ref[rs, :])
            for h in range(MLA_HEADS):
                hs = slice(h * HEAD_PAD, (h + 1) * HEAD_PAD)
                kt_s[r, hs, :] = (kv[:, hs] + kro).T.astype(BF16)
            v_s[rs, :] = (kv[:, QK_PAD:] + ones_hi).astype(BF16)
            return carry
        lax.fori_loop(0, N_ROW_BLK, blk, 0)

    def head_slice(h):
        return slice(h * HEAD_PAD, (h + 1) * HEAD_PAD)

    def row_max(blocks):
        mx = blocks[0]
        for sb in blocks[1:]:
            mx = jnp.maximum(mx, sb)
        return jnp.max(mx, axis=-1, keepdims=True)

    def attend(cq, rope, n_blk, o_ref):
        q = jnp.dot(_rms(cq, qn_ref[...]).astype(BF16), wuq_ref[...], preferred_element_type=F32)
        lane = lax.broadcasted_iota(jnp.int32, (q.shape[0], HEAD_PAD), 1)

        def roped_q(h):
            qh = q[:, head_slice(h)]
            qh = qh * Q_MUL if rope is None else _rope(qh, *rope)
            return qh.astype(BF16)

        score = lambda qh, h, r: jnp.dot(qh, kt_s[r, head_slice(h), :], preferred_element_type=F32)
        q_cur = roped_q(0)
        s_cur = [score(q_cur, 0, r) for r in range(n_blk)]
        m_cur = row_max(s_cur)
        pv = []
        for h in range(MLA_HEADS):
            more = h + 1 < MLA_HEADS
            if more:
                q_nxt = roped_q(h + 1)
            p_blocks, s_nxt = [], []
            for r in range(n_blk):
                p_blocks.append(jnp.exp2(s_cur[r] - m_cur).astype(BF16))
                if more:
                    s_nxt.append(score(q_nxt, h + 1, r))
            p = jnp.concatenate(p_blocks, axis=-1)
            pv.append(jnp.dot(p, v_s[0:n_blk * ROW_BLK, head_slice(h)], preferred_element_type=F32))
            if more:
                s_cur, m_cur = s_nxt, row_max(s_nxt)
            if h % 2 == 1:
                o_even = pv[h - 1] / pltpu.roll(pv[h - 1], MLA_V, 1)
                o_odd = pltpu.roll(pv[h], MLA_V, 1) / pv[h]
                o_ref[0, :, (h // 2) * LANES:(h // 2 + 1) * LANES] = jnp.where(
                    lane < MLA_V, o_even, o_odd).astype(o_ref.dtype)

    def latent_step():
        cq = cql_ref[...].reshape(Q_BLK, MLA_Q_RANK).astype(F32)
        attend(cq, (cosq_ref[...], sinq_ref[...]), N_ROW_BLK, ol_ref)

    if not with_ctx:
        latent_step()
        return

    @pl.when(j == 0)
    def _():
        attend(cqc_ref[...].astype(F32), None, CTX_LEN // ROW_BLK, oc_ref)

    @pl.when(j > 0)
    def _():
        latent_step()


def _attention(l, proj, cosq_lat, sinq_lat, cosk, sink, q_norm, w_uq_pad, kv_norm, w_ukv_pad, with_ctx):
    n_q = SEQ // Q_BLK
    const = lambda s: pl.BlockSpec(s, lambda b, j: (0,) * len(s))
    lat = (lambda j: jnp.maximum(j - 1, 0)) if with_ctx else (lambda j: j)
    proj4 = proj.reshape(BATCH, N_ROW_BLK, ROW_BLK, IN_PAD)
    blk_per_q = Q_BLK // ROW_BLK
    element = lambda *s: tuple(pl.Element(d) for d in s)
    in_specs = [
        pl.BlockSpec(element(1, blk_per_q, ROW_BLK, MLA_Q_RANK),
                     lambda b, j: (b, 1 + blk_per_q * lat(j), 0, PROJ_CQ * MLA_Q_RANK)),
        pl.BlockSpec((1, LT, 256), lambda b, j: (b, 0, PROJ_KVR)),
        pl.BlockSpec((Q_BLK, HEAD_PAD), lambda b, j: (lat(j), 0)),
        pl.BlockSpec((Q_BLK, HEAD_PAD), lambda b, j: (lat(j), 0)),
        const((LT, HEAD_PAD)),
        const((LT, HEAD_PAD)),
        _layer_spec(l, (1, MLA_Q_RANK)),
        _layer_spec(l, (MLA_Q_RANK, QK_PAD)),
        _layer_spec(l, (1, MLA_KV_RANK)),
        _layer_spec(l, (MLA_KV_RANK, 2 * QK_PAD)),
    ]
    args = [proj4, proj, cosq_lat, sinq_lat, cosk, sink, q_norm, w_uq_pad, kv_norm, w_ukv_pad]
    out_specs = [pl.BlockSpec((1, Q_BLK, MLA_WIDTH), lambda b, j: (b, lat(j), 0))]
    out_shape = [jax.ShapeDtypeStruct((BATCH, SEQ, MLA_WIDTH), BF16)]
    if with_ctx:
        in_specs.insert(0, pl.BlockSpec((None, ROW_BLK, MLA_Q_RANK), lambda b, j: (b, 0, PROJ_CQ)))
        args.insert(0, proj)
        out_specs.insert(0, pl.BlockSpec((1, CTX_LEN, MLA_WIDTH), lambda b, j: (b, 0, 0)))
        out_shape.insert(0, jax.ShapeDtypeStruct((BATCH, CTX_LEN, MLA_WIDTH), BF16))
    outs = pl.pallas_call(
        functools.partial(_attn_body, with_ctx=with_ctx),
        grid=(BATCH, n_q + (1 if with_ctx else 0)),
        in_specs=in_specs,
        out_specs=out_specs,
        out_shape=out_shape,
        scratch_shapes=[
            pltpu.VMEM((N_ROW_BLK, QK_PAD, ROW_BLK), BF16),
            pltpu.VMEM((LT, QK_PAD), BF16),
        ],
        compiler_params=pltpu.CompilerParams(vmem_limit_bytes=_VMEM_LIMIT),
        name="mla_attention",
    )(*args)
    return (outs[0], outs[1]) if with_ctx else (None, outs[0])


def _block_diag(blocks):
    n, r, c = blocks.shape
    tiled = jnp.tile(blocks.reshape(n * r, c), (1, n))
    on_diag = (jnp.arange(n * r)[:, None] // r) == (jnp.arange(n * c)[None, :] // c)
    return jnp.where(on_diag, tiled, jnp.zeros_like(tiled))


def _s5_matrices(lam_re, lam_im, log_dt, b_re, b_im, c_re, c_im):
    dt = jnp.exp(log_dt)[:, None]
    mag = jnp.exp(lam_re * dt)
    ab_re, ab_im = mag * jnp.cos(lam_im * dt), mag * jnp.sin(lam_im * dt)
    den = lam_re * lam_re + lam_im * lam_im
    nr = ab_re - 1.0
    coef_re = (nr * lam_re + ab_im * lam_im) / den
    coef_im = (ab_im * lam_re - nr * lam_im) / den
    bb_re = coef_re[..., None] * b_re - coef_im[..., None] * b_im
    bb_im = coef_re[..., None] * b_im + coef_im[..., None] * b_re
    to_in = lambda m: _block_diag(jnp.swapaxes(m, 1, 2))
    to_out = lambda m: _block_diag(jnp.swapaxes(m, 1, 2))

    def grouped(re, im, axis):
        parts = []
        for cb in range(S5_NSTATE // CBW):
            parts += [lax.slice_in_dim(re, cb * CBW, (cb + 1) * CBW, axis=axis),
                      lax.slice_in_dim(im, cb * CBW, (cb + 1) * CBW, axis=axis)]
        return jnp.concatenate(parts, axis=axis)

    bb = grouped(to_in(bb_re), to_in(bb_im), 1)
    cc = grouped(to_out(c_re), -to_out(c_im), 0)
    ab = jnp.stack([ab_re.reshape(-1), ab_im.reshape(-1)])
    return bb.astype(BF16), cc.astype(BF16), ab


def _rope_tables():
    n_rows = SEQ // GRID_W
    rows = jnp.repeat(jnp.arange(n_rows, dtype=F32), GRID_W)
    cols = jnp.tile(jnp.arange(GRID_W, dtype=F32), n_rows)
    inv_freq = ROPE_BASE ** (-jnp.arange(0, ROPE_AXIS, 2, dtype=F32) / ROPE_AXIS)
    ang_r = rows[:, None] * inv_freq
    ang_c = cols[:, None] * inv_freq
    cr, sr, cc, sc = jnp.cos(ang_r), jnp.sin(ang_r), jnp.cos(ang_c), jnp.sin(ang_c)
    cos32 = jnp.concatenate([cr, cr, cc, cc], axis=-1)
    sin32 = jnp.concatenate([-sr, sr, -sc, sc], axis=-1)
    cos = jnp.ones((LT, HEAD_PAD), F32).at[CTX_LEN:, MLA_NOPE:MLA_QK].set(cos32)
    sin = jnp.zeros((LT, HEAD_PAD), F32).at[CTX_LEN:, MLA_NOPE:MLA_QK].set(sin32)
    return cos, sin


def _pad_heads(w, used):
    w = jnp.pad(w, ((0, 0), (0, 0), (0, 0), (0, HEAD_PAD - used)))
    return w.reshape(w.shape[0], w.shape[1], MLA_HEADS * HEAD_PAD)


def kernel(x, c, ctx, c_ctx, ada_w, ada_b, norm_g, w_in, s5_lam_re, s5_lam_im, s5_log_dt, s5_b_re,
           s5_b_im, s5_c_re, s5_c_im, s5_d, s5_glu_w, s5_glu_b, lru_conv_w, lru_conv_b, lru_wa,
           lru_ba, lru_wx, lru_bx, lru_lam, mla_q_norm, mla_w_uq, mla_kv_norm, mla_w_ukv, w_out,
           final_g):
    c_pad = jnp.zeros((MOD_ROWS, D_MODEL), F32).at[:BATCH].set(c).at[BATCH].set(c_ctx)
    mod = _mod_tables(c_pad, ada_w, ada_b)

    cosk, sink = _rope_tables()
    cosq_lat, sinq_lat = cosk[CTX_LEN:] * Q_MUL, sink[CTX_LEN:] * Q_MUL

    kr_pad = jnp.zeros((DEPTH, D_MODEL, HEAD_PAD), F32).at[:, :, MLA_NOPE:MLA_QK].set(w_in[:, :, KR_LO:KR_HI])
    w_in_pad = jnp.concatenate([w_in[:, :, 0:KR_LO], kr_pad, w_in[:, :, KR_HI:D_IN]], axis=2).astype(BF16)
    bb, cc, ab = jax.vmap(jax.vmap(_s5_matrices))(s5_lam_re, s5_lam_im, s5_log_dt, s5_b_re, s5_b_im,
                                                  s5_c_re, s5_c_im)
    bd = jax.vmap(jax.vmap(_block_diag))
    wg = (0.5 * jnp.concatenate([bd(lru_wa), bd(lru_wx)], axis=-1)).astype(BF16)
    bg = 0.5 * jnp.concatenate([lru_ba, lru_bx], axis=-1)[:, :, None, :]
    sp = (0.5 * LRU_C) * jax.nn.softplus(-lru_lam)[:, :, None, :]
    w_uq_pad = _pad_heads(mla_w_uq.reshape(DEPTH, MLA_Q_RANK, MLA_HEADS, MLA_QK), MLA_QK).astype(BF16)
    ukv = mla_w_ukv.reshape(DEPTH, MLA_KV_RANK, MLA_HEADS, MLA_NOPE + MLA_V)
    w_ukv_pad = jnp.concatenate([_pad_heads(ukv[..., :MLA_NOPE], MLA_NOPE),
                                 _pad_heads(ukv[..., MLA_NOPE:], MLA_V)], axis=-1).astype(BF16)
    glu_w = s5_glu_w.astype(BF16)
    w_out_b = w_out.astype(BF16)
    row = lambda a: a[:, None, :]

    proj_w = (row(norm_g), w_in_pad)
    x_src = (ctx, x)
    (proj,) = _row_stage(0, x_src, mod, None, proj_w, None, merge=False, project=True)
    for l in range(DEPTH):
        final = l == DEPTH - 1
        y_fwd, y_bwd = (_scan(l, d, proj, bb, cc, ab, lru_conv_w, row(lru_conv_b), wg, bg, sp)
                        for d in range(2))
        ymla_ctx, ymla_lat = _attention(l, proj, cosq_lat, sinq_lat, cosk, sink, row(mla_q_norm), w_uq_pad,
                                        row(mla_kv_norm), w_ukv_pad, with_ctx=not final)
        merge_in = (y_fwd, y_bwd, proj, ymla_ctx, ymla_lat, row(s5_d), glu_w, row(s5_glu_b), w_out_b)
        if final:
            (out,) = _row_stage(l, x_src, mod, merge_in, None, final_g[None], merge=True,
                                project=False, final=True)
            return out
        x_all, proj = _row_stage(l, x_src, mod, merge_in, proj_w, None, merge=True, project=True)
        x_src = (x_all,)
```

```python
import functools
import math

import jax
import jax.numpy as jnp
from jax import lax
from jax.experimental import pallas as pl
from jax.experimental.pallas import tpu as pltpu

F32 = jnp.float32
BF16 = jnp.bfloat16

D_MODEL = 1024
BATCH = 16
SEQ = 2048
DEPTH = 4
GRID_W = 64
CTX_LEN = 256
LT = CTX_LEN + SEQ
S5_WIDTH = 256
S5_GROUPS = 16
S5_STATE = 64
S5_NSTATE = S5_GROUPS * S5_STATE
LRU_WIDTH = 256
LRU_CONV = 4
LRU_C = 8.0
MLA_HEADS = 8
MLA_NOPE = 64
MLA_QK = 96
MLA_V = 64
MLA_WIDTH = 512
MLA_Q_RANK = 256
MLA_KV_RANK = 128
ATTN_SCALE = MLA_QK ** -0.5
ROPE_AXIS = 16
ROPE_BASE = 10000.0
EPS = 1e-6

LANES = 128
HEAD_PAD = LANES
QK_PAD = MLA_HEADS * HEAD_PAD
IN_PAD = 2048
KR_LO, KR_HI, D_IN = 1408, 1440, 1952
MOD_ROWS = 24
ROW_BLK = 256
ROW_PAR = 4
N_ROW_BLK = LT // ROW_BLK
TL = 128
NCT = CTX_LEN // TL
NLT = SEQ // TL
NCH = NCT + NLT
HALO = 16
CBW = 512
MXU_TILE = 256
Q_BLK = 1024
Q_MUL = ATTN_SCALE * math.log2(math.e)

_VMEM_LIMIT = 56 * 1024 * 1024


def _split_bf16(v):
    hi = v.astype(BF16)
    lo = (v - hi.astype(F32)).astype(BF16)
    return hi, lo


def _sigmoid(v):
    return 0.5 * jnp.tanh(0.5 * v) + 0.5


def _silu(v):
    return v * _sigmoid(v)


def _rms(v, g):
    ms = jnp.mean(v * v, axis=-1, keepdims=True)
    return v * lax.rsqrt(ms + EPS) * g


def _layer_spec(l, shape):
    zeros = (0,) * len(shape)
    return pl.BlockSpec((None,) + shape, lambda a, b: (l,) + zeros)


def _mod_body(c_ref, w_ref, b_ref, o_ref):
    c = c_ref[...]
    sc = c * jax.nn.sigmoid(c)
    s_hi, s_lo = _split_bf16(sc)
    w_hi, w_lo = _split_bf16(w_ref[0])
    acc = jnp.dot(s_hi, w_hi, preferred_element_type=F32)
    acc += jnp.dot(s_hi, w_lo, preferred_element_type=F32)
    acc += jnp.dot(s_lo, w_hi, preferred_element_type=F32)
    o_ref[0] = acc + b_ref[0]


def _mod_tables(c_pad, ada_w, ada_b):
    nj = 3 * D_MODEL // 1024
    return pl.pallas_call(
        _mod_body,
        grid=(DEPTH, nj),
        in_specs=[
            pl.BlockSpec((MOD_ROWS, D_MODEL), lambda l, j: (0, 0)),
            pl.BlockSpec((1, D_MODEL, 1024), lambda l, j: (l, 0, j)),
            pl.BlockSpec((1, 1, 1024), lambda l, j: (l, 0, j)),
        ],
        out_specs=pl.BlockSpec((1, MOD_ROWS, 1024), lambda l, j: (l, 0, j)),
        out_shape=jax.ShapeDtypeStruct((DEPTH, MOD_ROWS, 3 * D_MODEL), F32),
        compiler_params=pltpu.CompilerParams(vmem_limit_bytes=_VMEM_LIMIT),
        name="adaln_tables",
    )(c_pad, ada_w, ada_b.reshape(DEPTH, 1, 3 * D_MODEL))


PROJ_US5, PROJ_XLRU, PROJ_ZLRU, PROJ_CQ, PROJ_KVR = 0, 2, 3, 4, 5


def _row_body(*refs, layer, merge, project, final, split_src):
    it = iter(refs)
    ctx_ref = next(it) if split_src else None
    x_ref, mod_ref = next(it), next(it)
    if merge:
        yf_ref, yb_ref, uz_ref, zlru_ref = (next(it) for _ in range(4))
        ymc_ref = None if final else next(it)
        yml_ref, zmla_ref, d_ref, gw_ref, gb_ref, wo_ref = (next(it) for _ in range(6))
    fg_ref = next(it) if final else None
    if project:
        g_ref, wi_ref = next(it), next(it)
    outs = list(it)

    b = pl.program_id(0)
    t = pl.program_id(1) + (1 if final else 0)
    is_ctx = t == 0
    rows = ROW_PAR * ROW_BLK

    def flat(ref):
        return ref[...].reshape(rows, ref.shape[-1])

    def mod_vec(layer_, p, lo):
        mod_row = jnp.where(is_ctx, BATCH, b * ROW_PAR + p)
        return mod_ref[layer_, pl.ds(mod_row, 1), lo:lo + D_MODEL]

    def per_sample(v, fn):
        return jnp.concatenate([fn(v[p * ROW_BLK:(p + 1) * ROW_BLK], p) for p in range(ROW_PAR)], axis=0)

    x = flat(x_ref)
    if split_src:
        x = jnp.where(is_ctx, flat(ctx_ref), x)

    if merge:
        ysc = flat(yf_ref).astype(F32) + flat(yb_ref).astype(F32)
        uz = flat(uz_ref).astype(F32)
        y = d_ref[...] * uz[:, 0:S5_WIDTH] + ysc[:, 0:S5_WIDTH]
        g = jnp.dot(y.astype(BF16), gw_ref[...], preferred_element_type=F32) + gb_ref[...]
        y_s5 = g[:, 0:S5_WIDTH] * _sigmoid(g[:, S5_WIDTH:])
        ymla = flat(yml_ref) if final else jnp.where(is_ctx, flat(ymc_ref), flat(yml_ref))
        cat = jnp.concatenate([
            (y_s5 * _silu(uz[:, S5_WIDTH:])).astype(BF16),
            (ysc[:, S5_WIDTH:] * _silu(flat(zlru_ref).astype(F32))).astype(BF16),
            (ymla.astype(F32) * _silu(flat(zmla_ref).astype(F32))).astype(BF16)], axis=-1)
        o = jnp.dot(cat, wo_ref[...], preferred_element_type=F32)
        x = x + per_sample(o, lambda v, p: mod_vec(layer, p, 2 * D_MODEL) * v)
        out = _rms(x, fg_ref[...]) if final else x
        out_ref = outs.pop(0)
        out_ref[...] = out.reshape(out_ref.shape)

    if project:
        nxt = layer + 1 if merge else layer
        h = per_sample(_rms(x, g_ref[...]),
                       lambda v, p: v * (1.0 + mod_vec(nxt, p, D_MODEL)) + mod_vec(nxt, p, 0))
        proj_ref = outs.pop(0)
        proj_ref[...] = jnp.dot(h.astype(BF16), wi_ref[...],
                                preferred_element_type=F32).astype(BF16).reshape(proj_ref.shape)


def _row_stage(layer, x_src, mod, merge_in, proj_w, final_g, *, merge, project, final=False):
    split_src = len(x_src) == 2
    t_off = 1 if final else 0
    row_spec = lambda w, cb=0: pl.BlockSpec((ROW_PAR, ROW_BLK, w), lambda b, t: (b, t + t_off, cb))
    const = lambda s: pl.BlockSpec(s, lambda b, t: (0,) * len(s))
    args, in_specs = [], []
    if split_src:
        args += list(x_src)
        in_specs += [pl.BlockSpec((ROW_PAR, ROW_BLK, D_MODEL), lambda b, t: (b, 0, 0)),
                     pl.BlockSpec((ROW_PAR, ROW_BLK, D_MODEL), lambda b, t: (b, jnp.maximum(t - 1, 0), 0))]
    else:
        args += list(x_src)
        in_specs += [row_spec(D_MODEL)]
    args.append(mod)
    in_specs.append(const((DEPTH, MOD_ROWS, 3 * D_MODEL)))
    if merge:
        y_fwd, y_bwd, proj, ymla_ctx, ymla_lat, s5_d, glu_w, glu_b, w_out = merge_in
        mla_args = [ymla_lat] if final else [ymla_ctx, ymla_lat]
        mla_block = (ROW_PAR, ROW_BLK, MLA_WIDTH)
        mla_specs = [pl.BlockSpec(mla_block, lambda b, t: (b, t, 0))] if final else [
            pl.BlockSpec(mla_block, lambda b, t: (b, 0, 0)),
            pl.BlockSpec(mla_block, lambda b, t: (b, jnp.maximum(t - 1, 0), 0))]
        args += [y_fwd, y_bwd, proj, proj, *mla_args, proj, s5_d, glu_w, glu_b, w_out]
        in_specs += [
            row_spec(512), row_spec(512),
            row_spec(512, 0),
            row_spec(256, PROJ_ZLRU),
            *mla_specs,
            row_spec(512, 3),
            _layer_spec(layer, (1, S5_WIDTH)),
            _layer_spec(layer, (S5_WIDTH, 2 * S5_WIDTH)),
            _layer_spec(layer, (1, 2 * S5_WIDTH)),
            _layer_spec(layer, (D_MODEL, D_MODEL)),
        ]
    if final:
        args.append(final_g)
        in_specs.append(const((1, D_MODEL)))
    if project:
        nxt = layer + 1 if merge else layer
        norm_g, w_in_pad = proj_w
        args += [norm_g, w_in_pad]
        in_specs += [_layer_spec(nxt, (1, D_MODEL)), _layer_spec(nxt, (D_MODEL, IN_PAD))]

    out_shape, out_specs, aliases = [], [], {}
    if merge:
        if final:
            out_shape.append(jax.ShapeDtypeStruct((BATCH, SEQ, D_MODEL), F32))
            out_specs.append(pl.BlockSpec((ROW_PAR, ROW_BLK, D_MODEL), lambda b, t: (b, t, 0)))
        else:
            out_shape.append(jax.ShapeDtypeStruct((BATCH, LT, D_MODEL), F32))
            out_specs.append(row_spec(D_MODEL))
            if not split_src:
                aliases = {0: 0}
    if project:
        out_shape.append(jax.ShapeDtypeStruct((BATCH, LT, IN_PAD), BF16))
        out_specs.append(row_spec(IN_PAD))
    return pl.pallas_call(
        functools.partial(_row_body, layer=layer, merge=merge, project=project, final=final,
                          split_src=split_src),
        grid=(BATCH // ROW_PAR, N_ROW_BLK - t_off),
        in_specs=in_specs,
        out_specs=out_specs,
        out_shape=out_shape,
        input_output_aliases=aliases,
        compiler_params=pltpu.CompilerParams(vmem_limit_bytes=_VMEM_LIMIT),
        name="row_stage",
    )(*args)


def _chunk_index(bwd, i):
    if not bwd:
        return i
    return jnp.where(i < NCT, NCT - 1 - i, 2 * NCT + NLT - 1 - i)


def _time_major(v):
    t = v.shape[1]
    return jnp.swapaxes(v, 0, 1).reshape(t * BATCH, v.shape[2])


def _batch_major(v):
    return jnp.swapaxes(v.reshape(v.shape[0] // BATCH, BATCH, v.shape[1]), 0, 1)


def _scan_body(us5_ref, xl_ref, xp_ref, xn_ref, bb_ref, cc_ref, ab_ref, cw_ref, cb_ref,
               wg_ref, bg_ref, sp_ref, ysc_ref,
               xs, hb, xpad, la, lb, hs5, hlru, *, bwd):
    i = pl.program_id(0)
    c = _chunk_index(bwd, i)
    rows = TL * BATCH
    half = rows // 2

    @pl.when(i == 0)
    def _():
        hs5[...] = jnp.zeros_like(hs5)
        hlru[...] = jnp.zeros_like(hlru)

    def rows_of(k):
        t = TL - 1 - k if bwd else k
        return slice(t * BATCH, (t + 1) * BATCH)

    def half_rows(k):
        first = (1 - k) if bwd else k
        return slice(first * half, (first + 1) * half)

    prev_ok = jnp.where((c != 0) & (c != NCT), 1.0, 0.0).astype(F32)
    next_ok = jnp.where((c != NCT - 1) & (c != NCH - 1), 1.0, 0.0).astype(F32)
    xpad[0:2 * BATCH, :] = _time_major(xp_ref[...])[(HALO - 2) * BATCH:, :].astype(F32) * prev_ok
    xpad[pl.ds(2 * BATCH, rows), :] = _time_major(xl_ref[...]).astype(F32)
    xpad[pl.ds((TL + 2) * BATCH, BATCH), :] = _time_major(xn_ref[...])[0:BATCH, :].astype(F32) * next_ok

    def gates(r0):
        rs = slice(r0, r0 + ROW_BLK)
        xc = cb_ref[...]
        for k in range(LRU_CONV):
            xc = xc + xpad[r0 + k * BATCH:r0 + k * BATCH + ROW_BLK, :] * cw_ref[k:k + 1, :]
        th = jnp.tanh(jnp.dot(xc.astype(BF16), wg_ref[...], preferred_element_type=F32) + bg_ref[...])
        sp4 = sp_ref[...]
        neg_log_a = sp4 * th[:, 0:LRU_WIDTH] + sp4
        a = jnp.exp2(neg_log_a * (-math.log2(math.e)))
        z = jnp.maximum(jnp.tanh(neg_log_a) * (a * a + 1.0), 0.0)
        mult = jnp.where(z > 0.0, z * lax.rsqrt(z), 0.0)
        la[rs, :] = a
        lb[rs, :] = (mult * xc) * (0.5 * th[:, LRU_WIDTH:] + 0.5)

    ub = _time_major(us5_ref[...])
    blk_w = 2 * CBW
    n_blk = S5_NSTATE // CBW
    n_tiles = 2 * S5_NSTATE // MXU_TILE
    half_steps = TL // 2

    def b_tile(k, n):
        hr, ns = half_rows(k), slice(n * MXU_TILE, (n + 1) * MXU_TILE)
        xs[hr, ns] = jnp.dot(ub[hr], bb_ref[:, ns], preferred_element_type=F32)

    def load_carry(cb):
        st = slice(cb * CBW, (cb + 1) * CBW)
        return (hs5[0, :, st], hs5[1, :, st]) + ((hlru[...],) if cb == 0 else ())

    def save_carry(cb, carry):
        st = slice(cb * CBW, (cb + 1) * CBW)
        hs5[0, :, st] = carry[0]
        hs5[1, :, st] = carry[1]
        if cb == 0:
            hlru[...] = carry[2]

    def scan_steps(cb, carry, k0, k1):
        st = slice(cb * CBW, (cb + 1) * CBW)
        re = slice(cb * blk_w, cb * blk_w + CBW)
        im = slice(cb * blk_w + CBW, (cb + 1) * blk_w)
        ar = jnp.broadcast_to(ab_ref[0:1, st], (BATCH, CBW))
        ai = jnp.broadcast_to(ab_ref[1:2, st], (BATCH, CBW))
        for k in range(k0, k1):
            rs = rows_of(k)
            hr, hi = carry[0], carry[1]
            nr = ar * hr - ai * hi + xs[rs, re]
            ni = ar * hi + ai * hr + xs[rs, im]
            hb[rs, re] = nr.astype(BF16)
            hb[rs, im] = ni.astype(BF16)
            if cb == 0:
                hl = la[rs, :] * carry[2] + lb[rs, :]
                lb[rs, :] = hl
                carry = (nr, ni, hl)
            else:
                carry = (nr, ni)
        return carry

    def c_map(k):
        return jnp.dot(hb[half_rows(k), :], cc_ref[...], preferred_element_type=F32)

    gate_blocks = list(range(rows // ROW_BLK))
    if bwd:
        gate_blocks.reverse()
    n_gate = len(gate_blocks)
    for n in range(n_tiles):
        b_tile(0, n)
        for g in gate_blocks[n * n_gate // n_tiles:(n + 1) * n_gate // n_tiles]:
            gates(g * ROW_BLK)
    tiles_per_blk = n_tiles // n_blk
    seg = half_steps // tiles_per_blk
    for cb in range(n_blk):
        carry = load_carry(cb)
        for n in range(tiles_per_blk):
            carry = scan_steps(cb, carry, n * seg, (n + 1) * seg)
            b_tile(1, cb * tiles_per_blk + n)
        save_carry(cb, carry)
    y_a = c_map(0)
    for cb in range(n_blk):
        save_carry(cb, scan_steps(cb, load_carry(cb), half_steps, TL))
    y_b = c_map(1)
    y = jnp.concatenate([y_b, y_a] if bwd else [y_a, y_b], axis=0)
    ysc_ref[:, :, 0:S5_WIDTH] = _batch_major(y.astype(ysc_ref.dtype))
    ysc_ref[:, :, S5_WIDTH:] = _batch_major(lb[...].astype(ysc_ref.dtype))


def _scan(l, d, proj, bb, cc, ab, conv_w, conv_b, wg, bg, sp):
    bwd = d == 1
    n_halo = LT // HALO
    chunk = lambda cb: (lambda i: (0, _chunk_index(bwd, i), cb))
    prev = lambda i: (0, jnp.maximum(_chunk_index(bwd, i) * (TL // HALO) - 1, 0), PROJ_XLRU)
    nxt = lambda i: (0, jnp.minimum((_chunk_index(bwd, i) + 1) * (TL // HALO), n_halo - 1), PROJ_XLRU)
    per_dir = lambda s: pl.BlockSpec((None, None) + s, lambda i: (l, d) + (0,) * len(s))
    per_layer = lambda s: pl.BlockSpec((None,) + s, lambda i: (l,) + (0,) * len(s))
    out_spec = pl.BlockSpec((BATCH, TL, 512), lambda i: (0, _chunk_index(bwd, i), 0))
    rows = TL * BATCH
    return pl.pallas_call(
        functools.partial(_scan_body, bwd=bwd),
        grid=(NCH,),
        in_specs=[
            pl.BlockSpec((BATCH, TL, S5_WIDTH), chunk(PROJ_US5)),
            pl.BlockSpec((BATCH, TL, LRU_WIDTH), chunk(PROJ_XLRU)),
            pl.BlockSpec((BATCH, HALO, LRU_WIDTH), prev),
            pl.BlockSpec((BATCH, HALO, LRU_WIDTH), nxt),
            per_dir((S5_WIDTH, 2 * S5_NSTATE)),
            per_dir((2 * S5_NSTATE, S5_WIDTH)),
            per_dir((2, S5_NSTATE)),
            per_layer((LRU_CONV, LRU_WIDTH)),
            per_layer((1, LRU_WIDTH)),
            per_dir((LRU_WIDTH, 2 * LRU_WIDTH)),
            per_dir((1, 2 * LRU_WIDTH)),
            per_dir((1, LRU_WIDTH)),
        ],
        out_specs=out_spec,
        out_shape=jax.ShapeDtypeStruct((BATCH, LT, 512), BF16),
        scratch_shapes=[
            pltpu.VMEM((rows, 2 * S5_NSTATE), F32),
            pltpu.VMEM((rows, 2 * S5_NSTATE), BF16),
            pltpu.VMEM(((TL + 3) * BATCH, LRU_WIDTH), F32),
            pltpu.VMEM((rows, LRU_WIDTH), F32),
            pltpu.VMEM((rows, LRU_WIDTH), F32),
            pltpu.VMEM((2, BATCH, S5_NSTATE), F32),
            pltpu.VMEM((BATCH, LRU_WIDTH), F32),
        ],
        compiler_params=pltpu.CompilerParams(vmem_limit_bytes=_VMEM_LIMIT),
        name="s5_lru_scan",
    )(proj, proj, proj, proj, bb, cc, ab, conv_w, conv_b, wg, bg, sp)


def _rope(v, cos, sin):
    width = v.shape[-1]
    lane = lax.broadcasted_iota(jnp.int32, v.shape, 1)
    first_half = ((lane // 8) % 2) == 0
    swapped = jnp.where(first_half, pltpu.roll(v, width - 8, 1), pltpu.roll(v, 8, 1))
    return v * cos + swapped * sin


def _attn_body(*refs, with_ctx):
    it = iter(refs)
    cqc_ref = next(it) if with_ctx else None
    (cql_ref, kvr_ref, cosq_ref, sinq_ref, cosk_ref, sink_ref, qn_ref, wuq_ref, kvn_ref,
     wukv_ref) = (next(it) for _ in range(10))
    oc_ref = next(it) if with_ctx else None
    ol_ref, kt_s, v_s = next(it), next(it), next(it)
    j = pl.program_id(1)

    @pl.when(j == 0)
    def _prep():
        lane = lax.broadcasted_iota(jnp.int32, (ROW_BLK, QK_PAD), 1)
        ones_hi = jnp.where(lane % HEAD_PAD >= MLA_V, 1.0, 0.0).astype(F32)

        def blk(r, carry):
            rs = pl.ds(pl.multiple_of(r * ROW_BLK, ROW_BLK), ROW_BLK)
            kvr = kvr_ref[0, rs, :].astype(F32)
            kvn = _rms(kvr[:, 0:MLA_KV_RANK], kvn_ref[...])
            kv = jnp.dot(kvn.astype(BF16), wukv_ref[...], preferred_element_type=F32)
            kro = _rope(kvr[:, MLA_KV_RANK:], cosk_ref[rs, :], sink_ref[rs, :])
            for h in range(MLA_HEADS):
                hs = slice(h * HEAD_PAD, (h + 1) * HEAD_PAD)
                kt_s[r, hs, :] = (kv[:, hs] + kro).T.astype(BF16)
            v_s[rs, :] = (kv[:, QK_PAD:] + ones_hi).astype(BF16)
            return carry
        lax.fori_loop(0, N_ROW_BLK, blk, 0)

    def head_slice(h):
        return slice(h * HEAD_PAD, (h + 1) * HEAD_PAD)

    def row_max(blocks):
        mx = blocks[0]
        for sb in blocks[1:]:
            mx = jnp.maximum(mx, sb)
        return jnp.max(mx, axis=-1, keepdims=True)

    def attend(cq, rope, n_blk, o_ref):
        q = jnp.dot(_rms(cq, qn_ref[...]).astype(BF16), wuq_ref[...], preferred_element_type=F32)
        lane = lax.broadcasted_iota(jnp.int32, (q.shape[0], HEAD_PAD), 1)

        def roped_q(h):
            qh = q[:, head_slice(h)]
            qh = qh * Q_MUL if rope is None else _rope(qh, *rope)
            return qh.astype(BF16)

        score = lambda qh, h, r: jnp.dot(qh, kt_s[r, head_slice(h), :], preferred_element_type=F32)
        q_cur = roped_q(0)
        s_cur = [score(q_cur, 0, r) for r in range(n_blk)]
        m_cur = row_max(s_cur)
        pv = []
        for h in range(MLA_HEADS):
            more = h + 1 < MLA_HEADS
            if more:
                q_nxt = roped_q(h + 1)
            p_blocks, s_nxt = [], []
            for r in range(n_blk):
                p_blocks.append(jnp.exp2(s_cur[r] - m_cur).astype(BF16))
                if more:
                    s_nxt.append(score(q_nxt, h + 1, r))
            p = jnp.concatenate(p_blocks, axis=-1)
            pv.append(jnp.dot(p, v_s[0:n_blk * ROW_BLK, head_slice(h)], preferred_element_type=F32))
            if more:
                s_cur, m_cur = s_nxt, row_max(s_nxt)
            if h % 2 == 1:
                o_even = pv[h - 1] / pltpu.roll(pv[h - 1], MLA_V, 1)
                o_odd = pltpu.roll(pv[h], MLA_V, 1) / pv[h]
                o_ref[0, :, (h // 2) * LANES:(h // 2 + 1) * LANES] = jnp.where(
                    lane < MLA_V, o_even, o_odd).astype(o_ref.dtype)

    def latent_step():
        cq = cql_ref[...].reshape(Q_BLK, MLA_Q_RANK).astype(F32)
        attend(cq, (cosq_ref[...], sinq_ref[...]), N_ROW_BLK, ol_ref)

    if not with_ctx:
        latent_step()
        return

    @pl.when(j == 0)
    def _():
        attend(cqc_ref[...].astype(F32), None, CTX_LEN // ROW_BLK, oc_ref)

    @pl.when(j > 0)
    def _():
        latent_step()


def _attention(l, proj, cosq_lat, sinq_lat, cosk, sink, q_norm, w_uq_pad, kv_norm, w_ukv_pad, with_ctx):
    n_q = SEQ // Q_BLK
    const = lambda s: pl.BlockSpec(s, lambda b, j: (0,) * len(s))
    lat = (lambda j: jnp.maximum(j - 1, 0)) if with_ctx else (lambda j: j)
    proj4 = proj.reshape(BATCH, N_ROW_BLK, ROW_BLK, IN_PAD)
    blk_per_q = Q_BLK // ROW_BLK
    element = lambda *s: tuple(pl.Element(d) for d in s)
    in_specs = [
        pl.BlockSpec(element(1, blk_per_q, ROW_BLK, MLA_Q_RANK),
                     lambda b, j: (b, 1 + blk_per_q * lat(j), 0, PROJ_CQ * MLA_Q_RANK)),
        pl.BlockSpec((1, LT, 256), lambda b, j: (b, 0, PROJ_KVR)),
        pl.BlockSpec((Q_BLK, HEAD_PAD), lambda b, j: (lat(j), 0)),
        pl.BlockSpec((Q_BLK, HEAD_PAD), lambda b, j: (lat(j), 0)),
        const((LT, HEAD_PAD)),
        const((LT, HEAD_PAD)),
        _layer_spec(l, (1, MLA_Q_RANK)),
        _layer_spec(l, (MLA_Q_RANK, QK_PAD)),
        _layer_spec(l, (1, MLA_KV_RANK)),
        _layer_spec(l, (MLA_KV_RANK, 2 * QK_PAD)),
    ]
    args = [proj4, proj, cosq_lat, sinq_lat, cosk, sink, q_norm, w_uq_pad, kv_norm, w_ukv_pad]
    out_specs = [pl.BlockSpec((1, Q_BLK, MLA_WIDTH), lambda b, j: (b, lat(j), 0))]
    out_shape = [jax.ShapeDtypeStruct((BATCH, SEQ, MLA_WIDTH), BF16)]
    if with_ctx:
        in_specs.insert(0, pl.BlockSpec((None, ROW_BLK, MLA_Q_RANK), lambda b, j: (b, 0, PROJ_CQ)))
        args.insert(0, proj)
        out_specs.insert(0, pl.BlockSpec((1, CTX_LEN, MLA_WIDTH), lambda b, j: (b, 0, 0)))
        out_shape.insert(0, jax.ShapeDtypeStruct((BATCH, CTX_LEN, MLA_WIDTH), BF16))
    outs = pl.pallas_call(
        functools.partial(_attn_body, with_ctx=with_ctx),
        grid=(BATCH, n_q + (1 if with_ctx else 0)),
        in_specs=in_specs,
        out_specs=out_specs,
        out_shape=out_shape,
        scratch_shapes=[
            pltpu.VMEM((N_ROW_BLK, QK_PAD, ROW_BLK), BF16),
            pltpu.VMEM((LT, QK_PAD), BF16),
        ],
        compiler_params=pltpu.CompilerParams(vmem_limit_bytes=_VMEM_LIMIT),
        name="mla_attention",
    )(*args)
    return (outs[0], outs[1]) if with_ctx else (None, outs[0])


def _block_diag(blocks):
    n, r, c = blocks.shape
    tiled = jnp.tile(blocks.reshape(n * r, c), (1, n))
    on_diag = (jnp.arange(n * r)[:, None] // r) == (jnp.arange(n * c)[None, :] // c)
    return jnp.where(on_diag, tiled, jnp.zeros_like(tiled))


def _s5_matrices(lam_re, lam_im, log_dt, b_re, b_im, c_re, c_im):
    dt = jnp.exp(log_dt)[:, None]
    mag = jnp.exp(lam_re * dt)
    ab_re, ab_im = mag * jnp.cos(lam_im * dt), mag * jnp.sin(lam_im * dt)
    den = lam_re * lam_re + lam_im * lam_im
    nr = ab_re - 1.0
    coef_re = (nr * lam_re + ab_im * lam_im) / den
    coef_im = (ab_im * lam_re - nr * lam_im) / den
    bb_re = coef_re[..., None] * b_re - coef_im[..., None] * b_im
    bb_im = coef_re[..., None] * b_im + coef_im[..., None] * b_re
    to_in = lambda m: _block_diag(jnp.swapaxes(m, 1, 2))
    to_out = lambda m: _block_diag(jnp.swapaxes(m, 1, 2))

    def grouped(re, im, axis):
        parts = []
        for cb in range(S5_NSTATE // CBW):
            parts += [lax.slice_in_dim(re, cb * CBW, (cb + 1) * CBW, axis=axis),
                      lax.slice_in_dim(im, cb * CBW, (cb + 1) * CBW, axis=axis)]
        return jnp.concatenate(parts, axis=axis)

    bb = grouped(to_in(bb_re), to_in(bb_im), 1)
    cc = grouped(to_out(c_re), -to_out(c_im), 0)
    ab = jnp.stack([ab_re.reshape(-1), ab_im.reshape(-1)])
    return bb.astype(BF16), cc.astype(BF16), ab


def _rope_tables():
    n_rows = SEQ // GRID_W
    rows = jnp.repeat(jnp.arange(n_rows, dtype=F32), GRID_W)
    cols = jnp.tile(jnp.arange(GRID_W, dtype=F32), n_rows)
    inv_freq = ROPE_BASE ** (-jnp.arange(0, ROPE_AXIS, 2, dtype=F32) / ROPE_AXIS)
    ang_r = rows[:, None] * inv_freq
    ang_c = cols[:, None] * inv_freq
    cr, sr, cc, sc = jnp.cos(ang_r), jnp.sin(ang_r), jnp.cos(ang_c), jnp.sin(ang_c)
    cos32 = jnp.concatenate([cr, cr, cc, cc], axis=-1)
    sin32 = jnp.concatenate([-sr, sr, -sc, sc], axis=-1)
    cos = jnp.ones((LT, HEAD_PAD), F32).at[CTX_LEN:, MLA_NOPE:MLA_QK].set(cos32)
    sin = jnp.zeros((LT, HEAD_PAD), F32).at[CTX_LEN:, MLA_NOPE:MLA_QK].set(sin32)
    return cos, sin


def _pad_heads(w, used):
    w = jnp.pad(w, ((0, 0), (0, 0), (0, 0), (0, HEAD_PAD - used)))
    return w.reshape(w.shape[0], w.shape[1], MLA_HEADS * HEAD_PAD)


def kernel(x, c, ctx, c_ctx, ada_w, ada_b, norm_g, w_in, s5_lam_re, s5_lam_im, s5_log_dt, s5_b_re,
           s5_b_im, s5_c_re, s5_c_im, s5_d, s5_glu_w, s5_glu_b, lru_conv_w, lru_conv_b, lru_wa,
           lru_ba, lru_wx, lru_bx, lru_lam, mla_q_norm, mla_w_uq, mla_kv_norm, mla_w_ukv, w_out,
           final_g):
    c_pad = jnp.zeros((MOD_ROWS, D_MODEL), F32).at[:BATCH].set(c).at[BATCH].set(c_ctx)
    mod = _mod_tables(c_pad, ada_w, ada_b)

    cosk, sink = _rope_tables()
    cosq_lat, sinq_lat = cosk[CTX_LEN:] * Q_MUL, sink[CTX_LEN:] * Q_MUL

    kr_pad = jnp.zeros((DEPTH, D_MODEL, HEAD_PAD), F32).at[:, :, MLA_NOPE:MLA_QK].set(w_in[:, :, KR_LO:KR_HI])
    w_in_pad = jnp.concatenate([w_in[:, :, 0:KR_LO], kr_pad, w_in[:, :, KR_HI:D_IN]], axis=2).astype(BF16)
    bb, cc, ab = jax.vmap(jax.vmap(_s5_matrices))(s5_lam_re, s5_lam_im, s5_log_dt, s5_b_re, s5_b_im,
                                                  s5_c_re, s5_c_im)
    bd = jax.vmap(jax.vmap(_block_diag))
    wg = (0.5 * jnp.concatenate([bd(lru_wa), bd(lru_wx)], axis=-1)).astype(BF16)
    bg = 0.5 * jnp.concatenate([lru_ba, lru_bx], axis=-1)[:, :, None, :]
    sp = (0.5 * LRU_C) * jax.nn.softplus(-lru_lam)[:, :, None, :]
    w_uq_pad = _pad_heads(mla_w_uq.reshape(DEPTH, MLA_Q_RANK, MLA_HEADS, MLA_QK), MLA_QK).astype(BF16)
    ukv = mla_w_ukv.reshape(DEPTH, MLA_KV_RANK, MLA_HEADS, MLA_NOPE + MLA_V)
    w_ukv_pad = jnp.concatenate([_pad_heads(ukv[..., :MLA_NOPE], MLA_NOPE),
                                 _pad_heads(ukv[..., MLA_NOPE:], MLA_V)], axis=-1).astype(BF16)
    glu_w = s5_glu_w.astype(BF16)
    w_out_b = w_out.astype(BF16)
    row = lambda a: a[:, None, :]

    proj_w = (row(norm_g), w_in_pad)
    x_src = (ctx, x)
    (proj,) = _row_stage(0, x_src, mod, None, proj_w, None, merge=False, project=True)
    for l in range(DEPTH):
        final = l == DEPTH - 1
        y_fwd, y_bwd = (_scan(l, d, proj, bb, cc, ab, lru_conv_w, row(lru_conv_b), wg, bg, sp)
                        for d in range(2))
        ymla_ctx, ymla_lat = _attention(l, proj, cosq_lat, sinq_lat, cosk, sink, row(mla_q_norm), w_uq_pad,
                                        row(mla_kv_norm), w_ukv_pad, with_ctx=not final)
        merge_in = (y_fwd, y_bwd, proj, ymla_ctx, ymla_lat, row(s5_d), glu_w, row(s5_glu_b), w_out_b)
        if final:
            (out,) = _row_stage(l, x_src, mod, merge_in, None, final_g[None], merge=True,
                                project=False, final=True)
            return out
        x_all, proj = _row_stage(l, x_src, mod, merge_in, proj_w, None, merge=True, project=True)
        x_src = (x_all,)
```

```python
import functools
import math

import jax
import jax.numpy as jnp
from jax import lax
from jax.experimental import pallas as pl
from jax.experimental.pallas import tpu as pltpu

F32 = jnp.float32
BF16 = jnp.bfloat16

D_MODEL = 1024
BATCH = 16
SEQ = 2048
DEPTH = 4
GRID_W = 64
CTX_LEN = 256
LT = CTX_LEN + SEQ
S5_WIDTH = 256
S5_GROUPS = 16
S5_STATE = 64
S5_NSTATE = S5_GROUPS * S5_STATE
LRU_WIDTH = 256
LRU_CONV = 4
LRU_C = 8.0
MLA_HEADS = 8
MLA_NOPE = 64
MLA_QK = 96
MLA_V = 64
MLA_WIDTH = 512
MLA_Q_RANK = 256
MLA_KV_RANK = 128
ATTN_SCALE = MLA_QK ** -0.5
ROPE_AXIS = 16
ROPE_BASE = 10000.0
EPS = 1e-6

LANES = 128
HEAD_PAD = LANES
QK_PAD = MLA_HEADS * HEAD_PAD
IN_PAD = 2048
KR_LO, KR_HI, D_IN = 1408, 1440, 1952
MOD_ROWS = 24
ROW_BLK = 256
ROW_PAR = 4
N_ROW_BLK = LT // ROW_BLK
TL = 128
NCT = CTX_LEN // TL
NLT = SEQ // TL
NCH = NCT + NLT
HALO = 16
CBW = 512
MXU_TILE = 256
Q_BLK = 1024
Q_MUL = ATTN_SCALE * math.log2(math.e)

_VMEM_LIMIT = 56 * 1024 * 1024


def _split_bf16(v):
    hi = v.astype(BF16)
    lo = (v - hi.astype(F32)).astype(BF16)
    return hi, lo


def _sigmoid(v):
    return 0.5 * jnp.tanh(0.5 * v) + 0.5


def _silu(v):
    return v * _sigmoid(v)


def _rms(v, g):
    ms = jnp.mean(v * v, axis=-1, keepdims=True)
    return v * lax.rsqrt(ms + EPS) * g


def _layer_spec(l, shape):
    zeros = (0,) * len(shape)
    return pl.BlockSpec((None,) + shape, lambda a, b: (l,) + zeros)


def _mod_body(c_ref, w_ref, b_ref, o_ref):
    c = c_ref[...]
    sc = c * jax.nn.sigmoid(c)
    s_hi, s_lo = _split_bf16(sc)
    w_hi, w_lo = _split_bf16(w_ref[0])
    acc = jnp.dot(s_hi, w_hi, preferred_element_type=F32)
    acc += jnp.dot(s_hi, w_lo, preferred_element_type=F32)
    acc += jnp.dot(s_lo, w_hi, preferred_element_type=F32)
    o_ref[0] = acc + b_ref[0]


def _mod_tables(c_pad, ada_w, ada_b):
    nj = 3 * D_MODEL // 1024
    return pl.pallas_call(
        _mod_body,
        grid=(DEPTH, nj),
        in_specs=[
            pl.BlockSpec((MOD_ROWS, D_MODEL), lambda l, j: (0, 0)),
            pl.BlockSpec((1, D_MODEL, 1024), lambda l, j: (l, 0, j)),
            pl.BlockSpec((1, 1, 1024), lambda l, j: (l, 0, j)),
        ],
        out_specs=pl.BlockSpec((1, MOD_ROWS, 1024), lambda l, j: (l, 0, j)),
        out_shape=jax.ShapeDtypeStruct((DEPTH, MOD_ROWS, 3 * D_MODEL), F32),
        compiler_params=pltpu.CompilerParams(vmem_limit_bytes=_VMEM_LIMIT),
        name="adaln_tables",
    )(c_pad, ada_w, ada_b.reshape(DEPTH, 1, 3 * D_MODEL))


PROJ_US5, PROJ_XLRU, PROJ_ZLRU, PROJ_CQ, PROJ_KVR = 0, 2, 3, 4, 5


def _row_body(*refs, layer, merge, project, final, split_src):
    it = iter(refs)
    ctx_ref = next(it) if split_src else None
    x_ref, mod_ref = next(it), next(it)
    if merge:
        yf_ref, yb_ref, uz_ref, zlru_ref = (next(it) for _ in range(4))
        ymc_ref = None if final else next(it)
        yml_ref, zmla_ref, d_ref, gw_ref, gb_ref, wo_ref = (next(it) for _ in range(6))
    fg_ref = next(it) if final else None
    if project:
        g_ref, wi_ref = next(it), next(it)
    outs = list(it)

    b = pl.program_id(0)
    t = pl.program_id(1) + (1 if final else 0)
    is_ctx = t == 0
    rows = ROW_PAR * ROW_BLK

    def flat(ref):
        return ref[...].reshape(rows, ref.shape[-1])

    def mod_vec(layer_, p, lo):
        mod_row = jnp.where(is_ctx, BATCH, b * ROW_PAR + p)
        return mod_ref[layer_, pl.ds(mod_row, 1), lo:lo + D_MODEL]

    def per_sample(v, fn):
        return jnp.concatenate([fn(v[p * ROW_BLK:(p + 1) * ROW_BLK], p) for p in range(ROW_PAR)], axis=0)

    x = flat(x_ref)
    if split_src:
        x = jnp.where(is_ctx, flat(ctx_ref), x)

    if merge:
        ysc = flat(yf_ref).astype(F32) + flat(yb_ref).astype(F32)
        uz = flat(uz_ref).astype(F32)
        y = d_ref[...] * uz[:, 0:S5_WIDTH] + ysc[:, 0:S5_WIDTH]
        g = jnp.dot(y.astype(BF16), gw_ref[...], preferred_element_type=F32) + gb_ref[...]
        y_s5 = g[:, 0:S5_WIDTH] * _sigmoid(g[:, S5_WIDTH:])
        ymla = flat(yml_ref) if final else jnp.where(is_ctx, flat(ymc_ref), flat(yml_ref))
        cat = jnp.concatenate([
            (y_s5 * _silu(uz[:, S5_WIDTH:])).astype(BF16),
            (ysc[:, S5_WIDTH:] * _silu(flat(zlru_ref).astype(F32))).astype(BF16),
            (ymla.astype(F32) * _silu(flat(zmla_ref).astype(F32))).astype(BF16)], axis=-1)
        o = jnp.dot(cat, wo_ref[...], preferred_element_type=F32)
        x = x + per_sample(o, lambda v, p: mod_vec(layer, p, 2 * D_MODEL) * v)
        out = _rms(x, fg_ref[...]) if final else x
        out_ref = outs.pop(0)
        out_ref[...] = out.reshape(out_ref.shape)

    if project:
        nxt = layer + 1 if merge else layer
        h = per_sample(_rms(x, g_ref[...]),
                       lambda v, p: v * (1.0 + mod_vec(nxt, p, D_MODEL)) + mod_vec(nxt, p, 0))
        proj_ref = outs.pop(0)
        proj_ref[...] = jnp.dot(h.astype(BF16), wi_ref[...],
                                preferred_element_type=F32).astype(BF16).reshape(proj_ref.shape)


def _row_stage(layer, x_src, mod, merge_in, proj_w, final_g, *, merge, project, final=False):
    split_src = len(x_src) == 2
    t_off = 1 if final else 0
    row_spec = lambda w, cb=0: pl.BlockSpec((ROW_PAR, ROW_BLK, w), lambda b, t: (b, t + t_off, cb))
    const = lambda s: pl.BlockSpec(s, lambda b, t: (0,) * len(s))
    args, in_specs = [], []
    if split_src:
        args += list(x_src)
        in_specs += [pl.BlockSpec((ROW_PAR, ROW_BLK, D_MODEL), lambda b, t: (b, 0, 0)),
                     pl.BlockSpec((ROW_PAR, ROW_BLK, D_MODEL), lambda b, t: (b, jnp.maximum(t - 1, 0), 0))]
    else:
        args += list(x_src)
        in_specs += [row_spec(D_MODEL)]
    args.append(mod)
    in_specs.append(const((DEPTH, MOD_ROWS, 3 * D_MODEL)))
    if merge:
        y_fwd, y_bwd, proj, ymla_ctx, ymla_lat, s5_d, glu_w, glu_b, w_out = merge_in
        mla_args = [ymla_lat] if final else [ymla_ctx, ymla_lat]
        mla_block = (ROW_PAR, ROW_BLK, MLA_WIDTH)
        mla_specs = [pl.BlockSpec(mla_block, lambda b, t: (b, t, 0))] if final else [
            pl.BlockSpec(mla_block, lambda b, t: (b, 0, 0)),
            pl.BlockSpec(mla_block, lambda b, t: (b, jnp.maximum(t - 1, 0), 0))]
        args += [y_fwd, y_bwd, proj, proj, *mla_args, proj, s5_d, glu_w, glu_b, w_out]
        in_specs += [
            row_spec(512), row_spec(512),
            row_spec(512, 0),
            row_spec(256, PROJ_ZLRU),
            *mla_specs,
            row_spec(512, 3),
            _layer_spec(layer, (1, S5_WIDTH)),
            _layer_spec(layer, (S5_WIDTH, 2 * S5_WIDTH)),
            _layer_spec(layer, (1, 2 * S5_WIDTH)),
            _layer_spec(layer, (D_MODEL, D_MODEL)),
        ]
    if final:
        args.append(final_g)
        in_specs.append(const((1, D_MODEL)))
    if project:
        nxt = layer + 1 if merge else layer
        norm_g, w_in_pad = proj_w
        args += [norm_g, w_in_pad]
        in_specs += [_layer_spec(nxt, (1, D_MODEL)), _layer_spec(nxt, (D_MODEL, IN_PAD))]

    out_shape, out_specs, aliases = [], [], {}
    if merge:
        if final:
            out_shape.append(jax.ShapeDtypeStruct((BATCH, SEQ, D_MODEL), F32))
            out_specs.append(pl.BlockSpec((ROW_PAR, ROW_BLK, D_MODEL), lambda b, t: (b, t, 0)))
        else:
            out_shape.append(jax.ShapeDtypeStruct((BATCH, LT, D_MODEL), F32))
            out_specs.append(row_spec(D_MODEL))
            if not split_src:
                aliases = {0: 0}
    if project:
        out_shape.append(jax.ShapeDtypeStruct((BATCH, LT, IN_PAD), BF16))
        out_specs.append(row_spec(IN_PAD))
    return pl.pallas_call(
        functools.partial(_row_body, layer=layer, merge=merge, project=project, final=final,
                          split_src=split_src),
        grid=(BATCH // ROW_PAR, N_ROW_BLK - t_off),
        in_specs=in_specs,
        out_specs=out_specs,
        out_shape=out_shape,
        input_output_aliases=aliases,
        compiler_params=pltpu.CompilerParams(vmem_limit_bytes=_VMEM_LIMIT),
        name="row_stage",
    )(*args)


def _chunk_index(bwd, i):
    if not bwd:
        return i
    return jnp.where(i < NCT, NCT - 1 - i, 2 * NCT + NLT - 1 - i)


def _time_major(v):
    t = v.shape[1]
    return jnp.swapaxes(v, 0, 1).reshape(t * BATCH, v.shape[2])


def _batch_major(v):
    return jnp.swapaxes(v.reshape(v.shape[0] // BATCH, BATCH, v.shape[1]), 0, 1)


def _scan_body(us5_ref, xl_ref, xp_ref, xn_ref, bb_ref, cc_ref, ab_ref, cw_ref, cb_ref,
               wg_ref, bg_ref, sp_ref, ysc_ref,
               xs, hb, xpad, la, lb, hs5, hlru, *, bwd):
    i = pl.program_id(0)
    c = _chunk_index(bwd, i)
    rows = TL * BATCH
    half = rows // 2

    @pl.when(i == 0)
    def _():
        hs5[...] = jnp.zeros_like(hs5)
        hlru[...] = jnp.zeros_like(hlru)

    def rows_of(k):
        t = TL - 1 - k if bwd else k
        return slice(t * BATCH, (t + 1) * BATCH)

    def half_rows(k):
        first = (1 - k) if bwd else k
        return slice(first * half, (first + 1) * half)

    prev_ok = jnp.where((c != 0) & (c != NCT), 1.0, 0.0).astype(F32)
    next_ok = jnp.where((c != NCT - 1) & (c != NCH - 1), 1.0, 0.0).astype(F32)
    xpad[0:2 * BATCH, :] = _time_major(xp_ref[...])[(HALO - 2) * BATCH:, :].astype(F32) * prev_ok
    xpad[pl.ds(2 * BATCH, rows), :] = _time_major(xl_ref[...]).astype(F32)
    xpad[pl.ds((TL + 2) * BATCH, BATCH), :] = _time_major(xn_ref[...])[0:BATCH, :].astype(F32) * next_ok

    def gates(r0):
        rs = slice(r0, r0 + ROW_BLK)
        xc = cb_ref[...]
        for k in range(LRU_CONV):
            xc = xc + xpad[r0 + k * BATCH:r0 + k * BATCH + ROW_BLK, :] * cw_ref[k:k + 1, :]
        th = jnp.tanh(jnp.dot(xc.astype(BF16), wg_ref[...], preferred_element_type=F32) + bg_ref[...])
        sp4 = sp_ref[...]
        neg_log_a = sp4 * th[:, 0:LRU_WIDTH] + sp4
        a = jnp.exp2(neg_log_a * (-math.log2(math.e)))
        z = jnp.maximum(jnp.tanh(neg_log_a) * (a * a + 1.0), 0.0)
        mult = jnp.where(z > 0.0, z * lax.rsqrt(z), 0.0)
        la[rs, :] = a
        lb[rs, :] = (mult * xc) * (0.5 * th[:, LRU_WIDTH:] + 0.5)

    ub = _time_major(us5_ref[...])
    blk_w = 2 * CBW
    n_blk = S5_NSTATE // CBW
    n_tiles = 2 * S5_NSTATE // MXU_TILE
    half_steps = TL // 2

    def b_tile(k, n):
        hr, ns = half_rows(k), slice(n * MXU_TILE, (n + 1) * MXU_TILE)
        xs[hr, ns] = jnp.dot(ub[hr], bb_ref[:, ns], preferred_element_type=F32)

    def load_carry(cb):
        st = slice(cb * CBW, (cb + 1) * CBW)
        return (hs5[0, :, st], hs5[1, :, st]) + ((hlru[...],) if cb == 0 else ())

    def save_carry(cb, carry):
        st = slice(cb * CBW, (cb + 1) * CBW)
        hs5[0, :, st] = carry[0]
        hs5[1, :, st] = carry[1]
        if cb == 0:
            hlru[...] = carry[2]

    def scan_steps(cb, carry, k0, k1):
        st = slice(cb * CBW, (cb + 1) * CBW)
        re = slice(cb * blk_w, cb * blk_w + CBW)
        im = slice(cb * blk_w + CBW, (cb + 1) * blk_w)
        ar = jnp.broadcast_to(ab_ref[0:1, st], (BATCH, CBW))
        ai = jnp.broadcast_to(ab_ref[1:2, st], (BATCH, CBW))
        for k in range(k0, k1):
            rs = rows_of(k)
            hr, hi = carry[0], carry[1]
            nr = ar * hr - ai * hi + xs[rs, re]
            ni = ar * hi + ai * hr + xs[rs, im]
            hb[rs, re] = nr.astype(BF16)
            hb[rs, im] = ni.astype(BF16)
            if cb == 0:
                hl = la[rs, :] * carry[2] + lb[rs, :]
                lb[rs, :] = hl
                carry = (nr, ni, hl)
            else:
                carry = (nr, ni)
        return carry

    def c_map(k):
        return jnp.dot(hb[half_rows(k), :], cc_ref[...], preferred_element_type=F32)

    gate_blocks = list(range(rows // ROW_BLK))
    if bwd:
        gate_blocks.reverse()
    n_gate = len(gate_blocks)
    for n in range(n_tiles):
        b_tile(0, n)
        for g in gate_blocks[n * n_gate // n_tiles:(n + 1) * n_gate // n_tiles]:
            gates(g * ROW_BLK)
    tiles_per_blk = n_tiles // n_blk
    seg = half_steps // tiles_per_blk
    for cb in range(n_blk):
        carry = load_carry(cb)
        for n in range(tiles_per_blk):
            carry = scan_steps(cb, carry, n * seg, (n + 1) * seg)
            b_tile(1, cb * tiles_per_blk + n)
        save_carry(cb, carry)
    y_a = c_map(0)
    for cb in range(n_blk):
        save_carry(cb, scan_steps(cb, load_carry(cb), half_steps, TL))
    y_b = c_map(1)
    y = jnp.concatenate([y_b, y_a] if bwd else [y_a, y_b], axis=0)
    ysc_ref[:, :, 0:S5_WIDTH] = _batch_major(y.astype(ysc_ref.dtype))
    ysc_ref[:, :, S5_WIDTH:] = _batch_major(lb[...].astype(ysc_ref.dtype))


def _scan(l, d, proj, bb, cc, ab, conv_w, conv_b, wg, bg, sp):
    bwd = d == 1
    n_halo = LT // HALO
    chunk = lambda cb: (lambda i: (0, _chunk_index(bwd, i), cb))
    prev = lambda i: (0, jnp.maximum(_chunk_index(bwd, i) * (TL // HALO) - 1, 0), PROJ_XLRU)
    nxt = lambda i: (0, jnp.minimum((_chunk_index(bwd, i) + 1) * (TL // HALO), n_halo - 1), PROJ_XLRU)
    per_dir = lambda s: pl.BlockSpec((None, None) + s, lambda i: (l, d) + (0,) * len(s))
    per_layer = lambda s: pl.BlockSpec((None,) + s, lambda i: (l,) + (0,) * len(s))
    out_spec = pl.BlockSpec((BATCH, TL, 512), lambda i: (0, _chunk_index(bwd, i), 0))
    rows = TL * BATCH
    return pl.pallas_call(
        functools.partial(_scan_body, bwd=bwd),
        grid=(NCH,),
        in_specs=[
            pl.BlockSpec((BATCH, TL, S5_WIDTH), chunk(PROJ_US5)),
            pl.BlockSpec((BATCH, TL, LRU_WIDTH), chunk(PROJ_XLRU)),
            pl.BlockSpec((BATCH, HALO, LRU_WIDTH), prev),
            pl.BlockSpec((BATCH, HALO, LRU_WIDTH), nxt),
            per_dir((S5_WIDTH, 2 * S5_NSTATE)),
            per_dir((2 * S5_NSTATE, S5_WIDTH)),
            per_dir((2, S5_NSTATE)),
            per_layer((LRU_CONV, LRU_WIDTH)),
            per_layer((1, LRU_WIDTH)),
            per_dir((LRU_WIDTH, 2 * LRU_WIDTH)),
            per_dir((1, 2 * LRU_WIDTH)),
            per_dir((1, LRU_WIDTH)),
        ],
        out_specs=out_spec,
        out_shape=jax.ShapeDtypeStruct((BATCH, LT, 512), BF16),
        scratch_shapes=[
            pltpu.VMEM((rows, 2 * S5_NSTATE), F32),
            pltpu.VMEM((rows, 2 * S5_NSTATE), BF16),
            pltpu.VMEM(((TL + 3) * BATCH, LRU_WIDTH), F32),
            pltpu.VMEM((rows, LRU_WIDTH), F32),
            pltpu.VMEM((rows, LRU_WIDTH), F32),
            pltpu.VMEM((2, BATCH, S5_NSTATE), F32),
            pltpu.VMEM((BATCH, LRU_WIDTH), F32),
        ],
        compiler_params=pltpu.CompilerParams(vmem_limit_bytes=_VMEM_LIMIT),
        name="s5_lru_scan",
    )(proj, proj, proj, proj, bb, cc, ab, conv_w, conv_b, wg, bg, sp)


def _rope(v, cos, sin):
    width = v.shape[-1]
    lane = lax.broadcasted_iota(jnp.int32, v.shape, 1)
    first_half = ((lane // 8) % 2) == 0
    swapped = jnp.where(first_half, pltpu.roll(v, width - 8, 1), pltpu.roll(v, 8, 1))
    return v * cos + swapped * sin


def _attn_body(*refs, with_ctx):
    it = iter(refs)
    cqc_ref = next(it) if with_ctx else None
    (cql_ref, kvr_ref, cosq_ref, sinq_ref, cosk_ref, sink_ref, qn_ref, wuq_ref, kvn_ref,
     wukv_ref) = (next(it) for _ in range(10))
    oc_ref = next(it) if with_ctx else None
    ol_ref, kt_s, v_s = next(it), next(it), next(it)
    j = pl.program_id(1)

    @pl.when(j == 0)
    def _prep():
        lane = lax.broadcasted_iota(jnp.int32, (ROW_BLK, HEAD_PAD), 1)
        ones_hi = jnp.where(lane >= MLA_V, 1.0, 0.0).astype(F32)

        def blk(r, carry):
            rs = pl.ds(pl.multiple_of(r * ROW_BLK, ROW_BLK), ROW_BLK)
            kvr = kvr_ref[0, rs, :].astype(F32)
            kvn = _rms(kvr[:, 0:MLA_KV_RANK], kvn_ref[...])
            kv = jnp.dot(kvn.astype(BF16), wukv_ref[...], preferred_element_type=F32)
            kro = _rope(kvr[:, MLA_KV_RANK:], cosk_ref[rs, :], sink_ref[rs, :])
            for h in range(MLA_HEADS):
                hs = slice(h * HEAD_PAD, (h + 1) * HEAD_PAD)
                kt_s[rs, hs] = (kv[:, hs] + kro).astype(BF16)
                v_s[r, hs, :] = (kv[:, QK_PAD + h * HEAD_PAD:QK_PAD + (h + 1) * HEAD_PAD]
                                 + ones_hi).T.astype(BF16)
            return carry
        lax.fori_loop(0, N_ROW_BLK, blk, 0)

    def head_slice(h):
        return slice(h * HEAD_PAD, (h + 1) * HEAD_PAD)

    def row_max(blocks):
        mx = blocks[0]
        for sb in blocks[1:]:
            mx = jnp.maximum(mx, sb)
        return jnp.max(mx, axis=0, keepdims=True)

    def attend(cq, rope, n_blk, o_ref):
        q = jnp.dot(_rms(cq, qn_ref[...]).astype(BF16), wuq_ref[...], preferred_element_type=F32)
        lane = lax.broadcasted_iota(jnp.int32, (q.shape[0], HEAD_PAD), 1)

        def roped_q(h):
            qh = q[:, head_slice(h)]
            qh = qh * Q_MUL if rope is None else _rope(qh, *rope)
            return qh.T.astype(BF16)

        score = lambda qt, h, r: jnp.dot(kt_s[r * ROW_BLK:(r + 1) * ROW_BLK, head_slice(h)], qt,
                                         preferred_element_type=F32)
        q_cur = roped_q(0)
        s_cur = [score(q_cur, 0, r) for r in range(n_blk)]
        m_cur = row_max(s_cur)
        pv = []
        for h in range(MLA_HEADS):
            more = h + 1 < MLA_HEADS
            if more:
                q_nxt = roped_q(h + 1)
            p_blocks, s_nxt = [], []
            for r in range(n_blk):
                p_blocks.append(jnp.exp2(s_cur[r] - m_cur).astype(BF16))
                if more:
                    s_nxt.append(score(q_nxt, h + 1, r))
            p_t = jnp.concatenate(p_blocks, axis=0)
            v_t = jnp.concatenate([v_s[r, head_slice(h), :] for r in range(n_blk)], axis=1)
            pv.append(jnp.dot(v_t, p_t, preferred_element_type=F32).T)
            if more:
                s_cur, m_cur = s_nxt, row_max(s_nxt)
            if h % 2 == 1:
                o_even = pv[h - 1] / pltpu.roll(pv[h - 1], MLA_V, 1)
                o_odd = pltpu.roll(pv[h], MLA_V, 1) / pv[h]
                o_ref[0, :, (h // 2) * LANES:(h // 2 + 1) * LANES] = jnp.where(
                    lane < MLA_V, o_even, o_odd).astype(o_ref.dtype)

    def latent_step():
        cq = cql_ref[...].reshape(Q_BLK, MLA_Q_RANK).astype(F32)
        attend(cq, (cosq_ref[...], sinq_ref[...]), N_ROW_BLK, ol_ref)

    if not with_ctx:
        latent_step()
        return

    @pl.when(j == 0)
    def _():
        attend(cqc_ref[...].astype(F32), None, CTX_LEN // ROW_BLK, oc_ref)

    @pl.when(j > 0)
    def _():
        latent_step()


def _attention(l, proj, cosq_lat, sinq_lat, cosk, sink, q_norm, w_uq_pad, kv_norm, w_ukv_pad, with_ctx):
    n_q = SEQ // Q_BLK
    const = lambda s: pl.BlockSpec(s, lambda b, j: (0,) * len(s))
    lat = (lambda j: jnp.maximum(j - 1, 0)) if with_ctx else (lambda j: j)
    proj4 = proj.reshape(BATCH, N_ROW_BLK, ROW_BLK, IN_PAD)
    blk_per_q = Q_BLK // ROW_BLK
    element = lambda *s: tuple(pl.Element(d) for d in s)
    in_specs = [
        pl.BlockSpec(element(1, blk_per_q, ROW_BLK, MLA_Q_RANK),
                     lambda b, j: (b, 1 + blk_per_q * lat(j), 0, PROJ_CQ * MLA_Q_RANK)),
        pl.BlockSpec((1, LT, 256), lambda b, j: (b, 0, PROJ_KVR)),
        pl.BlockSpec((Q_BLK, HEAD_PAD), lambda b, j: (lat(j), 0)),
        pl.BlockSpec((Q_BLK, HEAD_PAD), lambda b, j: (lat(j), 0)),
        const((LT, HEAD_PAD)),
        const((LT, HEAD_PAD)),
        _layer_spec(l, (1, MLA_Q_RANK)),
        _layer_spec(l, (MLA_Q_RANK, QK_PAD)),
        _layer_spec(l, (1, MLA_KV_RANK)),
        _layer_spec(l, (MLA_KV_RANK, 2 * QK_PAD)),
    ]
    args = [proj4, proj, cosq_lat, sinq_lat, cosk, sink, q_norm, w_uq_pad, kv_norm, w_ukv_pad]
    out_specs = [pl.BlockSpec((1, Q_BLK, MLA_WIDTH), lambda b, j: (b, lat(j), 0))]
    out_shape = [jax.ShapeDtypeStruct((BATCH, SEQ, MLA_WIDTH), BF16)]
    if with_ctx:
        in_specs.insert(0, pl.BlockSpec((None, ROW_BLK, MLA_Q_RANK), lambda b, j: (b, 0, PROJ_CQ)))
        args.insert(0, proj)
        out_specs.insert(0, pl.BlockSpec((1, CTX_LEN, MLA_WIDTH), lambda b, j: (b, 0, 0)))
        out_shape.insert(0, jax.ShapeDtypeStruct((BATCH, CTX_LEN, MLA_WIDTH), BF16))
    outs = pl.pallas_call(
        functools.partial(_attn_body, with_ctx=with_ctx),
        grid=(BATCH, n_q + (1 if with_ctx else 0)),
        in_specs=in_specs,
        out_specs=out_specs,
        out_shape=out_shape,
        scratch_shapes=[
            pltpu.VMEM((LT, QK_PAD), BF16),
            pltpu.VMEM((N_ROW_BLK, QK_PAD, ROW_BLK), BF16),
        ],
        compiler_params=pltpu.CompilerParams(vmem_limit_bytes=_VMEM_LIMIT),
        name="mla_attention",
    )(*args)
    return (outs[0], outs[1]) if with_ctx else (None, outs[0])


def _block_diag(blocks):
    n, r, c = blocks.shape
    tiled = jnp.tile(blocks.reshape(n * r, c), (1, n))
    on_diag = (jnp.arange(n * r)[:, None] // r) == (jnp.arange(n * c)[None, :] // c)
    return jnp.where(on_diag, tiled, jnp.zeros_like(tiled))


def _s5_matrices(lam_re, lam_im, log_dt, b_re, b_im, c_re, c_im):
    dt = jnp.exp(log_dt)[:, None]
    mag = jnp.exp(lam_re * dt)
    ab_re, ab_im = mag * jnp.cos(lam_im * dt), mag * jnp.sin(lam_im * dt)
    den = lam_re * lam_re + lam_im * lam_im
    nr = ab_re - 1.0
    coef_re = (nr * lam_re + ab_im * lam_im) / den
    coef_im = (ab_im * lam_re - nr * lam_im) / den
    bb_re = coef_re[..., None] * b_re - coef_im[..., None] * b_im
    bb_im = coef_re[..., None] * b_im + coef_im[..., None] * b_re
    to_in = lambda m: _block_diag(jnp.swapaxes(m, 1, 2))
    to_out = lambda m: _block_diag(jnp.swapaxes(m, 1, 2))

    def grouped(re, im, axis):
        parts = []
        for cb in range(S5_NSTATE // CBW):
            parts += [lax.slice_in_dim(re, cb * CBW, (cb + 1) * CBW, axis=axis),
                      lax.slice_in_dim(im, cb * CBW, (cb + 1) * CBW, axis=axis)]
        return jnp.concatenate(parts, axis=axis)

    bb = grouped(to_in(bb_re), to_in(bb_im), 1)
    cc = grouped(to_out(c_re), -to_out(c_im), 0)
    ab = jnp.stack([ab_re.reshape(-1), ab_im.reshape(-1)])
    return bb.astype(BF16), cc.astype(BF16), ab


def _rope_tables():
    n_rows = SEQ // GRID_W
    rows = jnp.repeat(jnp.arange(n_rows, dtype=F32), GRID_W)
    cols = jnp.tile(jnp.arange(GRID_W, dtype=F32), n_rows)
    inv_freq = ROPE_BASE ** (-jnp.arange(0, ROPE_AXIS, 2, dtype=F32) / ROPE_AXIS)
    ang_r = rows[:, None] * inv_freq
    ang_c = cols[:, None] * inv_freq
    cr, sr, cc, sc = jnp.cos(ang_r), jnp.sin(ang_r), jnp.cos(ang_c), jnp.sin(ang_c)
    cos32 = jnp.concatenate([cr, cr, cc, cc], axis=-1)
    sin32 = jnp.concatenate([-sr, sr, -sc, sc], axis=-1)
    cos = jnp.ones((LT, HEAD_PAD), F32).at[CTX_LEN:, MLA_NOPE:MLA_QK].set(cos32)
    sin = jnp.zeros((LT, HEAD_PAD), F32).at[CTX_LEN:, MLA_NOPE:MLA_QK].set(sin32)
    return cos, sin


def _pad_heads(w, used):
    w = jnp.pad(w, ((0, 0), (0, 0), (0, 0), (0, HEAD_PAD - used)))
    return w.reshape(w.shape[0], w.shape[1], MLA_HEADS * HEAD_PAD)


def kernel(x, c, ctx, c_ctx, ada_w, ada_b, norm_g, w_in, s5_lam_re, s5_lam_im, s5_log_dt, s5_b_re,
           s5_b_im, s5_c_re, s5_c_im, s5_d, s5_glu_w, s5_glu_b, lru_conv_w, lru_conv_b, lru_wa,
           lru_ba, lru_wx, lru_bx, lru_lam, mla_q_norm, mla_w_uq, mla_kv_norm, mla_w_ukv, w_out,
           final_g):
    c_pad = jnp.zeros((MOD_ROWS, D_MODEL), F32).at[:BATCH].set(c).at[BATCH].set(c_ctx)
    mod = _mod_tables(c_pad, ada_w, ada_b)

    cosk, sink = _rope_tables()
    cosq_lat, sinq_lat = cosk[CTX_LEN:] * Q_MUL, sink[CTX_LEN:] * Q_MUL

    kr_pad = jnp.zeros((DEPTH, D_MODEL, HEAD_PAD), F32).at[:, :, MLA_NOPE:MLA_QK].set(w_in[:, :, KR_LO:KR_HI])
    w_in_pad = jnp.concatenate([w_in[:, :, 0:KR_LO], kr_pad, w_in[:, :, KR_HI:D_IN]], axis=2).astype(BF16)
    bb, cc, ab = jax.vmap(jax.vmap(_s5_matrices))(s5_lam_re, s5_lam_im, s5_log_dt, s5_b_re, s5_b_im,
                                                  s5_c_re, s5_c_im)
    bd = jax.vmap(jax.vmap(_block_diag))
    wg = (0.5 * jnp.concatenate([bd(lru_wa), bd(lru_wx)], axis=-1)).astype(BF16)
    bg = 0.5 * jnp.concatenate([lru_ba, lru_bx], axis=-1)[:, :, None, :]
    sp = (0.5 * LRU_C) * jax.nn.softplus(-lru_lam)[:, :, None, :]
    w_uq_pad = _pad_heads(mla_w_uq.reshape(DEPTH, MLA_Q_RANK, MLA_HEADS, MLA_QK), MLA_QK).astype(BF16)
    ukv = mla_w_ukv.reshape(DEPTH, MLA_KV_RANK, MLA_HEADS, MLA_NOPE + MLA_V)
    w_ukv_pad = jnp.concatenate([_pad_heads(ukv[..., :MLA_NOPE], MLA_NOPE),
                                 _pad_heads(ukv[..., MLA_NOPE:], MLA_V)], axis=-1).astype(BF16)
    glu_w = s5_glu_w.astype(BF16)
    w_out_b = w_out.astype(BF16)
    row = lambda a: a[:, None, :]

    proj_w = (row(norm_g), w_in_pad)
    x_src = (ctx, x)
    (proj,) = _row_stage(0, x_src, mod, None, proj_w, None, merge=False, project=True)
    for l in range(DEPTH):
        final = l == DEPTH - 1
        y_fwd, y_bwd = (_scan(l, d, proj, bb, cc, ab, lru_conv_w, row(lru_conv_b), wg, bg, sp)
                        for d in range(2))
        ymla_ctx, ymla_lat = _attention(l, proj, cosq_lat, sinq_lat, cosk, sink, row(mla_q_norm), w_uq_pad,
                                        row(mla_kv_norm), w_ukv_pad, with_ctx=not final)
        merge_in = (y_fwd, y_bwd, proj, ymla_ctx, ymla_lat, row(s5_d), glu_w, row(s5_glu_b), w_out_b)
        if final:
            (out,) = _row_stage(l, x_src, mod, merge_in, None, final_g[None], merge=True,
                                project=False, final=True)
            return out
        x_all, proj = _row_stage(l, x_src, mod, merge_in, proj_w, None, merge=True, project=True)
        x_src = (x_all,)
```

```python
import functools
import math

import jax
import jax.numpy as jnp
from jax import lax
from jax.experimental import pallas as pl
from jax.experimental.pallas import tpu as pltpu

F32 = jnp.float32
BF16 = jnp.bfloat16

D_MODEL = 1024
BATCH = 16
SEQ = 2048
DEPTH = 4
GRID_W = 64
CTX_LEN = 256
LT = CTX_LEN + SEQ
S5_WIDTH = 256
S5_GROUPS = 16
S5_STATE = 64
S5_NSTATE = S5_GROUPS * S5_STATE
LRU_WIDTH = 256
LRU_CONV = 4
LRU_C = 8.0
MLA_HEADS = 8
MLA_NOPE = 64
MLA_QK = 96
MLA_V = 64
MLA_WIDTH = 512
MLA_Q_RANK = 256
MLA_KV_RANK = 128
ATTN_SCALE = MLA_QK ** -0.5
ROPE_AXIS = 16
ROPE_BASE = 10000.0
EPS = 1e-6

LANES = 128
HEAD_PAD = LANES
QK_PAD = MLA_HEADS * HEAD_PAD
IN_PAD = 2048
KR_LO, KR_HI, D_IN = 1408, 1440, 1952
MOD_ROWS = 24
ROW_BLK = 256
ROW_PAR = 4
N_ROW_BLK = LT // ROW_BLK
TL = 128
NCT = CTX_LEN // TL
NLT = SEQ // TL
NCH = NCT + NLT
HALO = 16
CBW = 512
MXU_TILE = 256
Q_BLK = 1024
PREP_BLKS = 3
Q_MUL = ATTN_SCALE * math.log2(math.e)

_VMEM_LIMIT = 56 * 1024 * 1024


def _split_bf16(v):
    hi = v.astype(BF16)
    lo = (v - hi.astype(F32)).astype(BF16)
    return hi, lo


def _sigmoid(v):
    return 0.5 * jnp.tanh(0.5 * v) + 0.5


def _silu(v):
    return v * _sigmoid(v)


def _rms(v, g):
    ms = jnp.mean(v * v, axis=-1, keepdims=True)
    return v * lax.rsqrt(ms + EPS) * g


def _layer_spec(l, shape):
    zeros = (0,) * len(shape)
    return pl.BlockSpec((None,) + shape, lambda a, b: (l,) + zeros)


def _mod_body(c_ref, w_ref, b_ref, o_ref):
    c = c_ref[...]
    sc = c * jax.nn.sigmoid(c)
    s_hi, s_lo = _split_bf16(sc)
    w_hi, w_lo = _split_bf16(w_ref[0])
    acc = jnp.dot(s_hi, w_hi, preferred_element_type=F32)
    acc += jnp.dot(s_hi, w_lo, preferred_element_type=F32)
    acc += jnp.dot(s_lo, w_hi, preferred_element_type=F32)
    o_ref[0] = acc + b_ref[0]


def _mod_tables(c_pad, ada_w, ada_b):
    nj = 3 * D_MODEL // 1024
    return pl.pallas_call(
        _mod_body,
        grid=(DEPTH, nj),
        in_specs=[
            pl.BlockSpec((MOD_ROWS, D_MODEL), lambda l, j: (0, 0)),
            pl.BlockSpec((1, D_MODEL, 1024), lambda l, j: (l, 0, j)),
            pl.BlockSpec((1, 1, 1024), lambda l, j: (l, 0, j)),
        ],
        out_specs=pl.BlockSpec((1, MOD_ROWS, 1024), lambda l, j: (l, 0, j)),
        out_shape=jax.ShapeDtypeStruct((DEPTH, MOD_ROWS, 3 * D_MODEL), F32),
        compiler_params=pltpu.CompilerParams(vmem_limit_bytes=_VMEM_LIMIT),
        name="adaln_tables",
    )(c_pad, ada_w, ada_b.reshape(DEPTH, 1, 3 * D_MODEL))


PROJ_US5, PROJ_XLRU, PROJ_ZLRU, PROJ_CQ, PROJ_KVR = 0, 2, 3, 4, 5


def _row_body(*refs, layer, merge, project, final, split_src):
    it = iter(refs)
    ctx_ref = next(it) if split_src else None
    x_ref, mod_ref = next(it), next(it)
    if merge:
        yf_ref, yb_ref, uz_ref, zlru_ref = (next(it) for _ in range(4))
        ymc_ref = None if final else next(it)
        yml_ref, zmla_ref, d_ref, gw_ref, gb_ref, wo_ref = (next(it) for _ in range(6))
    fg_ref = next(it) if final else None
    if project:
        g_ref, wi_ref = next(it), next(it)
    outs = list(it)

    b = pl.program_id(0)
    t = pl.program_id(1) + (1 if final else 0)
    is_ctx = t == 0
    rows = ROW_PAR * ROW_BLK

    def flat(ref):
        return ref[...].reshape(rows, ref.shape[-1])

    def mod_vec(layer_, p, lo):
        mod_row = jnp.where(is_ctx, BATCH, b * ROW_PAR + p)
        return mod_ref[layer_, pl.ds(mod_row, 1), lo:lo + D_MODEL]

    def per_sample(v, fn):
        return jnp.concatenate([fn(v[p * ROW_BLK:(p + 1) * ROW_BLK], p) for p in range(ROW_PAR)], axis=0)

    x = flat(x_ref)
    if split_src:
        x = jnp.where(is_ctx, flat(ctx_ref), x)

    if merge:
        ysc = flat(yf_ref).astype(F32) + flat(yb_ref).astype(F32)
        uz = flat(uz_ref).astype(F32)
        y = d_ref[...] * uz[:, 0:S5_WIDTH] + ysc[:, 0:S5_WIDTH]
        g = jnp.dot(y.astype(BF16), gw_ref[...], preferred_element_type=F32) + gb_ref[...]
        y_s5 = g[:, 0:S5_WIDTH] * _sigmoid(g[:, S5_WIDTH:])
        ymla = flat(yml_ref) if final else jnp.where(is_ctx, flat(ymc_ref), flat(yml_ref))
        cat = jnp.concatenate([
            (y_s5 * _silu(uz[:, S5_WIDTH:])).astype(BF16),
            (ysc[:, S5_WIDTH:] * _silu(flat(zlru_ref).astype(F32))).astype(BF16),
            (ymla.astype(F32) * _silu(flat(zmla_ref).astype(F32))).astype(BF16)], axis=-1)
        o = jnp.dot(cat, wo_ref[...], preferred_element_type=F32)
        x = x + per_sample(o, lambda v, p: mod_vec(layer, p, 2 * D_MODEL) * v)
        out = _rms(x, fg_ref[...]) if final else x
        out_ref = outs.pop(0)
        out_ref[...] = out.reshape(out_ref.shape)

    if project:
        nxt = layer + 1 if merge else layer
        h = per_sample(_rms(x, g_ref[...]),
                       lambda v, p: v * (1.0 + mod_vec(nxt, p, D_MODEL)) + mod_vec(nxt, p, 0))
        proj_ref = outs.pop(0)
        proj_ref[...] = jnp.dot(h.astype(BF16), wi_ref[...],
                                preferred_element_type=F32).astype(BF16).reshape(proj_ref.shape)


def _row_stage(layer, x_src, mod, merge_in, proj_w, final_g, *, merge, project, final=False):
    split_src = len(x_src) == 2
    t_off = 1 if final else 0
    row_spec = lambda w, cb=0: pl.BlockSpec((ROW_PAR, ROW_BLK, w), lambda b, t: (b, t + t_off, cb))
    const = lambda s: pl.BlockSpec(s, lambda b, t: (0,) * len(s))
    args, in_specs = [], []
    if split_src:
        args += list(x_src)
        in_specs += [pl.BlockSpec((ROW_PAR, ROW_BLK, D_MODEL), lambda b, t: (b, 0, 0)),
                     pl.BlockSpec((ROW_PAR, ROW_BLK, D_MODEL), lambda b, t: (b, jnp.maximum(t - 1, 0), 0))]
    else:
        args += list(x_src)
        in_specs += [row_spec(D_MODEL)]
    args.append(mod)
    in_specs.append(const((DEPTH, MOD_ROWS, 3 * D_MODEL)))
    if merge:
        y_fwd, y_bwd, proj, ymla_ctx, ymla_lat, s5_d, glu_w, glu_b, w_out = merge_in
        mla_args = [ymla_lat] if final else [ymla_ctx, ymla_lat]
        mla_block = (ROW_PAR, ROW_BLK, MLA_WIDTH)
        mla_specs = [pl.BlockSpec(mla_block, lambda b, t: (b, t, 0))] if final else [
            pl.BlockSpec(mla_block, lambda b, t: (b, 0, 0)),
            pl.BlockSpec(mla_block, lambda b, t: (b, jnp.maximum(t - 1, 0), 0))]
        args += [y_fwd, y_bwd, proj, proj, *mla_args, proj, s5_d, glu_w, glu_b, w_out]
        in_specs += [
            row_spec(512), row_spec(512),
            row_spec(512, 0),
            row_spec(256, PROJ_ZLRU),
            *mla_specs,
            row_spec(512, 3),
            _layer_spec(layer, (1, S5_WIDTH)),
            _layer_spec(layer, (S5_WIDTH, 2 * S5_WIDTH)),
            _layer_spec(layer, (1, 2 * S5_WIDTH)),
            _layer_spec(layer, (D_MODEL, D_MODEL)),
        ]
    if final:
        args.append(final_g)
        in_specs.append(const((1, D_MODEL)))
    if project:
        nxt = layer + 1 if merge else layer
        norm_g, w_in_pad = proj_w
        args += [norm_g, w_in_pad]
        in_specs += [_layer_spec(nxt, (1, D_MODEL)), _layer_spec(nxt, (D_MODEL, IN_PAD))]

    out_shape, out_specs, aliases = [], [], {}
    if merge:
        if final:
            out_shape.append(jax.ShapeDtypeStruct((BATCH, SEQ, D_MODEL), F32))
            out_specs.append(pl.BlockSpec((ROW_PAR, ROW_BLK, D_MODEL), lambda b, t: (b, t, 0)))
        else:
            out_shape.append(jax.ShapeDtypeStruct((BATCH, LT, D_MODEL), F32))
            out_specs.append(row_spec(D_MODEL))
            if not split_src:
                aliases = {0: 0}
    if project:
        out_shape.append(jax.ShapeDtypeStruct((BATCH, LT, IN_PAD), BF16))
        out_specs.append(row_spec(IN_PAD))
    return pl.pallas_call(
        functools.partial(_row_body, layer=layer, merge=merge, project=project, final=final,
                          split_src=split_src),
        grid=(BATCH // ROW_PAR, N_ROW_BLK - t_off),
        in_specs=in_specs,
        out_specs=out_specs,
        out_shape=out_shape,
        input_output_aliases=aliases,
        compiler_params=pltpu.CompilerParams(vmem_limit_bytes=_VMEM_LIMIT),
        name="row_stage",
    )(*args)


def _chunk_index(bwd, i):
    if not bwd:
        return i
    return jnp.where(i < NCT, NCT - 1 - i, 2 * NCT + NLT - 1 - i)


def _time_major(v):
    t = v.shape[1]
    return jnp.swapaxes(v, 0, 1).reshape(t * BATCH, v.shape[2])


def _batch_major(v):
    return jnp.swapaxes(v.reshape(v.shape[0] // BATCH, BATCH, v.shape[1]), 0, 1)


def _scan_body(us5_ref, xl_ref, xp_ref, xn_ref, bb_ref, cc_ref, ab_ref, cw_ref, cb_ref,
               wg_ref, bg_ref, sp_ref, ysc_ref,
               xs, hb, xpad, la, lb, hs5, hlru, *, bwd):
    i = pl.program_id(0)
    c = _chunk_index(bwd, i)
    rows = TL * BATCH
    half = rows // 2

    @pl.when(i == 0)
    def _():
        hs5[...] = jnp.zeros_like(hs5)
        hlru[...] = jnp.zeros_like(hlru)

    def rows_of(k):
        t = TL - 1 - k if bwd else k
        return slice(t * BATCH, (t + 1) * BATCH)

    def half_rows(k):
        first = (1 - k) if bwd else k
        return slice(first * half, (first + 1) * half)

    prev_ok = jnp.where((c != 0) & (c != NCT), 1.0, 0.0).astype(F32)
    next_ok = jnp.where((c != NCT - 1) & (c != NCH - 1), 1.0, 0.0).astype(F32)
    xpad[0:2 * BATCH, :] = _time_major(xp_ref[...])[(HALO - 2) * BATCH:, :].astype(F32) * prev_ok
    xpad[pl.ds(2 * BATCH, rows), :] = _time_major(xl_ref[...]).astype(F32)
    xpad[pl.ds((TL + 2) * BATCH, BATCH), :] = _time_major(xn_ref[...])[0:BATCH, :].astype(F32) * next_ok

    def gates(r0):
        rs = slice(r0, r0 + ROW_BLK)
        xc = cb_ref[...]
        for k in range(LRU_CONV):
            xc = xc + xpad[r0 + k * BATCH:r0 + k * BATCH + ROW_BLK, :] * cw_ref[k:k + 1, :]
        th = jnp.tanh(jnp.dot(xc.astype(BF16), wg_ref[...], preferred_element_type=F32) + bg_ref[...])
        sp4 = sp_ref[...]
        neg_log_a = sp4 * th[:, 0:LRU_WIDTH] + sp4
        a = jnp.exp2(neg_log_a * (-math.log2(math.e)))
        z = jnp.maximum(jnp.tanh(neg_log_a) * (a * a + 1.0), 0.0)
        mult = jnp.where(z > 0.0, z * lax.rsqrt(z), 0.0)
        la[rs, :] = a
        lb[rs, :] = (mult * xc) * (0.5 * th[:, LRU_WIDTH:] + 0.5)

    ub = _time_major(us5_ref[...])
    blk_w = 2 * CBW
    n_blk = S5_NSTATE // CBW
    n_tiles = 2 * S5_NSTATE // MXU_TILE
    half_steps = TL // 2

    def b_tile(k, n):
        hr, ns = half_rows(k), slice(n * MXU_TILE, (n + 1) * MXU_TILE)
        xs[hr, ns] = jnp.dot(ub[hr], bb_ref[:, ns], preferred_element_type=F32)

    def load_carry(cb):
        st = slice(cb * CBW, (cb + 1) * CBW)
        return (hs5[0, :, st], hs5[1, :, st]) + ((hlru[...],) if cb == 0 else ())

    def save_carry(cb, carry):
        st = slice(cb * CBW, (cb + 1) * CBW)
        hs5[0, :, st] = carry[0]
        hs5[1, :, st] = carry[1]
        if cb == 0:
            hlru[...] = carry[2]

    def scan_steps(cb, carry, k0, k1):
        st = slice(cb * CBW, (cb + 1) * CBW)
        re = slice(cb * blk_w, cb * blk_w + CBW)
        im = slice(cb * blk_w + CBW, (cb + 1) * blk_w)
        ar = jnp.broadcast_to(ab_ref[0:1, st], (BATCH, CBW))
        ai = jnp.broadcast_to(ab_ref[1:2, st], (BATCH, CBW))
        for k in range(k0, k1):
            rs = rows_of(k)
            hr, hi = carry[0], carry[1]
            nr = ar * hr - ai * hi + xs[rs, re]
            ni = ar * hi + ai * hr + xs[rs, im]
            hb[rs, re] = nr.astype(BF16)
            hb[rs, im] = ni.astype(BF16)
            if cb == 0:
                hl = la[rs, :] * carry[2] + lb[rs, :]
                lb[rs, :] = hl
                carry = (nr, ni, hl)
            else:
                carry = (nr, ni)
        return carry

    def c_map(k):
        return jnp.dot(hb[half_rows(k), :], cc_ref[...], preferred_element_type=F32)

    gate_blocks = list(range(rows // ROW_BLK))
    if bwd:
        gate_blocks.reverse()
    n_gate = len(gate_blocks)
    for n in range(n_tiles):
        b_tile(0, n)
        for g in gate_blocks[n * n_gate // n_tiles:(n + 1) * n_gate // n_tiles]:
            gates(g * ROW_BLK)
    tiles_per_blk = n_tiles // n_blk
    seg = half_steps // tiles_per_blk
    for cb in range(n_blk):
        carry = load_carry(cb)
        for n in range(tiles_per_blk):
            carry = scan_steps(cb, carry, n * seg, (n + 1) * seg)
            b_tile(1, cb * tiles_per_blk + n)
        save_carry(cb, carry)
    y_a = c_map(0)
    for cb in range(n_blk):
        save_carry(cb, scan_steps(cb, load_carry(cb), half_steps, TL))
    y_b = c_map(1)
    y = jnp.concatenate([y_b, y_a] if bwd else [y_a, y_b], axis=0)
    ysc_ref[:, :, 0:S5_WIDTH] = _batch_major(y.astype(ysc_ref.dtype))
    ysc_ref[:, :, S5_WIDTH:] = _batch_major(lb[...].astype(ysc_ref.dtype))


def _scan(l, d, proj, bb, cc, ab, conv_w, conv_b, wg, bg, sp):
    bwd = d == 1
    n_halo = LT // HALO
    chunk = lambda cb: (lambda i: (0, _chunk_index(bwd, i), cb))
    prev = lambda i: (0, jnp.maximum(_chunk_index(bwd, i) * (TL // HALO) - 1, 0), PROJ_XLRU)
    nxt = lambda i: (0, jnp.minimum((_chunk_index(bwd, i) + 1) * (TL // HALO), n_halo - 1), PROJ_XLRU)
    per_dir = lambda s: pl.BlockSpec((None, None) + s, lambda i: (l, d) + (0,) * len(s))
    per_layer = lambda s: pl.BlockSpec((None,) + s, lambda i: (l,) + (0,) * len(s))
    out_spec = pl.BlockSpec((BATCH, TL, 512), lambda i: (0, _chunk_index(bwd, i), 0))
    rows = TL * BATCH
    return pl.pallas_call(
        functools.partial(_scan_body, bwd=bwd),
        grid=(NCH,),
        in_specs=[
            pl.BlockSpec((BATCH, TL, S5_WIDTH), chunk(PROJ_US5)),
            pl.BlockSpec((BATCH, TL, LRU_WIDTH), chunk(PROJ_XLRU)),
            pl.BlockSpec((BATCH, HALO, LRU_WIDTH), prev),
            pl.BlockSpec((BATCH, HALO, LRU_WIDTH), nxt),
            per_dir((S5_WIDTH, 2 * S5_NSTATE)),
            per_dir((2 * S5_NSTATE, S5_WIDTH)),
            per_dir((2, S5_NSTATE)),
            per_layer((LRU_CONV, LRU_WIDTH)),
            per_layer((1, LRU_WIDTH)),
            per_dir((LRU_WIDTH, 2 * LRU_WIDTH)),
            per_dir((1, 2 * LRU_WIDTH)),
            per_dir((1, LRU_WIDTH)),
        ],
        out_specs=out_spec,
        out_shape=jax.ShapeDtypeStruct((BATCH, LT, 512), BF16),
        scratch_shapes=[
            pltpu.VMEM((rows, 2 * S5_NSTATE), F32),
            pltpu.VMEM((rows, 2 * S5_NSTATE), BF16),
            pltpu.VMEM(((TL + 3) * BATCH, LRU_WIDTH), F32),
            pltpu.VMEM((rows, LRU_WIDTH), F32),
            pltpu.VMEM((rows, LRU_WIDTH), F32),
            pltpu.VMEM((2, BATCH, S5_NSTATE), F32),
            pltpu.VMEM((BATCH, LRU_WIDTH), F32),
        ],
        compiler_params=pltpu.CompilerParams(vmem_limit_bytes=_VMEM_LIMIT),
        name="s5_lru_scan",
    )(proj, proj, proj, proj, bb, cc, ab, conv_w, conv_b, wg, bg, sp)


def _rope(v, cos, sin):
    width = v.shape[-1]
    lane = lax.broadcasted_iota(jnp.int32, v.shape, 1)
    first_half = ((lane // 8) % 2) == 0
    swapped = jnp.where(first_half, pltpu.roll(v, width - 8, 1), pltpu.roll(v, 8, 1))
    return v * cos + swapped * sin


def _attn_body(*refs, with_ctx):
    it = iter(refs)
    cqc_ref = next(it) if with_ctx else None
    (cql_ref, kvr_ref, cosq_ref, sinq_ref, cosk_ref, sink_ref, qn_ref, wuq_ref, kvn_ref,
     wukv_ref) = (next(it) for _ in range(10))
    oc_ref = next(it) if with_ctx else None
    ol_ref, kt_s, v_s = next(it), next(it), next(it)
    j = pl.program_id(1)

    @pl.when(j == 0)
    def _prep():
        prep_rows = PREP_BLKS * ROW_BLK
        lane = lax.broadcasted_iota(jnp.int32, (prep_rows, QK_PAD), 1)
        ones_hi = jnp.where(lane % HEAD_PAD >= MLA_V, 1.0, 0.0).astype(F32)

        def blk(r, carry):
            rs = pl.ds(pl.multiple_of(r * prep_rows, prep_rows), prep_rows)
            kvr = kvr_ref[0, rs, :].astype(F32)
            kvn = _rms(kvr[:, 0:MLA_KV_RANK], kvn_ref[...])
            kv = jnp.dot(kvn.astype(BF16), wukv_ref[...], preferred_element_type=F32)
            kro = _rope(kvr[:, MLA_KV_RANK:], cosk_ref[rs, :], sink_ref[rs, :])
            for sub in range(PREP_BLKS):
                ss = slice(sub * ROW_BLK, (sub + 1) * ROW_BLK)
                for h in range(MLA_HEADS):
                    hs = slice(h * HEAD_PAD, (h + 1) * HEAD_PAD)
                    kt_s[r * PREP_BLKS + sub, hs, :] = (kv[ss, hs] + kro[ss]).T.astype(BF16)
            v_s[rs, :] = (kv[:, QK_PAD:] + ones_hi).astype(BF16)
            return carry
        lax.fori_loop(0, N_ROW_BLK // PREP_BLKS, blk, 0)

    def head_slice(h):
        return slice(h * HEAD_PAD, (h + 1) * HEAD_PAD)

    def row_max(blocks):
        mx = blocks[0]
        for sb in blocks[1:]:
            mx = jnp.maximum(mx, sb)
        return jnp.max(mx, axis=-1, keepdims=True)

    def attend(cq, rope, n_blk, o_ref):
        q = jnp.dot(_rms(cq, qn_ref[...]).astype(BF16), wuq_ref[...], preferred_element_type=F32)
        lane = lax.broadcasted_iota(jnp.int32, (q.shape[0], HEAD_PAD), 1)

        def roped_q(h):
            qh = q[:, head_slice(h)]
            qh = qh * Q_MUL if rope is None else _rope(qh, *rope)
            return qh.astype(BF16)

        score = lambda qh, h, r: jnp.dot(qh, kt_s[r, head_slice(h), :], preferred_element_type=F32)
        q_cur = roped_q(0)
        s_cur = [score(q_cur, 0, r) for r in range(n_blk)]
        m_cur = row_max(s_cur)
        pv = []
        for h in range(MLA_HEADS):
            more = h + 1 < MLA_HEADS
            if more:
                q_nxt = roped_q(h + 1)
            p_blocks, s_nxt = [], []
            for r in range(n_blk):
                p_blocks.append(jnp.exp2(s_cur[r] - m_cur).astype(BF16))
                if more:
                    s_nxt.append(score(q_nxt, h + 1, r))
            p = jnp.concatenate(p_blocks, axis=-1)
            pv.append(jnp.dot(p, v_s[0:n_blk * ROW_BLK, head_slice(h)], preferred_element_type=F32))
            if more:
                s_cur, m_cur = s_nxt, row_max(s_nxt)
            if h % 2 == 1:
                o_even = pv[h - 1] / pltpu.roll(pv[h - 1], MLA_V, 1)
                o_odd = pltpu.roll(pv[h], MLA_V, 1) / pv[h]
                o_ref[0, :, (h // 2) * LANES:(h // 2 + 1) * LANES] = jnp.where(
                    lane < MLA_V, o_even, o_odd).astype(o_ref.dtype)

    def latent_step():
        cq = cql_ref[...].reshape(Q_BLK, MLA_Q_RANK).astype(F32)
        attend(cq, (cosq_ref[...], sinq_ref[...]), N_ROW_BLK, ol_ref)

    if not with_ctx:
        latent_step()
        return

    @pl.when(j == 0)
    def _():
        attend(cqc_ref[...].astype(F32), None, CTX_LEN // ROW_BLK, oc_ref)

    @pl.when(j > 0)
    def _():
        latent_step()


def _attention(l, proj, cosq_lat, sinq_lat, cosk, sink, q_norm, w_uq_pad, kv_norm, w_ukv_pad, with_ctx):
    n_q = SEQ // Q_BLK
    const = lambda s: pl.BlockSpec(s, lambda b, j: (0,) * len(s))
    lat = (lambda j: jnp.maximum(j - 1, 0)) if with_ctx else (lambda j: j)
    proj4 = proj.reshape(BATCH, N_ROW_BLK, ROW_BLK, IN_PAD)
    blk_per_q = Q_BLK // ROW_BLK
    element = lambda *s: tuple(pl.Element(d) for d in s)
    in_specs = [
        pl.BlockSpec(element(1, blk_per_q, ROW_BLK, MLA_Q_RANK),
                     lambda b, j: (b, 1 + blk_per_q * lat(j), 0, PROJ_CQ * MLA_Q_RANK)),
        pl.BlockSpec((1, LT, 256), lambda b, j: (b, 0, PROJ_KVR)),
        pl.BlockSpec((Q_BLK, HEAD_PAD), lambda b, j: (lat(j), 0)),
        pl.BlockSpec((Q_BLK, HEAD_PAD), lambda b, j: (lat(j), 0)),
        const((LT, HEAD_PAD)),
        const((LT, HEAD_PAD)),
        _layer_spec(l, (1, MLA_Q_RANK)),
        _layer_spec(l, (MLA_Q_RANK, QK_PAD)),
        _layer_spec(l, (1, MLA_KV_RANK)),
        _layer_spec(l, (MLA_KV_RANK, 2 * QK_PAD)),
    ]
    args = [proj4, proj, cosq_lat, sinq_lat, cosk, sink, q_norm, w_uq_pad, kv_norm, w_ukv_pad]
    out_specs = [pl.BlockSpec((1, Q_BLK, MLA_WIDTH), lambda b, j: (b, lat(j), 0))]
    out_shape = [jax.ShapeDtypeStruct((BATCH, SEQ, MLA_WIDTH), BF16)]
    if with_ctx:
        in_specs.insert(0, pl.BlockSpec((None, ROW_BLK, MLA_Q_RANK), lambda b, j: (b, 0, PROJ_CQ)))
        args.insert(0, proj)
        out_specs.insert(0, pl.BlockSpec((1, CTX_LEN, MLA_WIDTH), lambda b, j: (b, 0, 0)))
        out_shape.insert(0, jax.ShapeDtypeStruct((BATCH, CTX_LEN, MLA_WIDTH), BF16))
    outs = pl.pallas_call(
        functools.partial(_attn_body, with_ctx=with_ctx),
        grid=(BATCH, n_q + (1 if with_ctx else 0)),
        in_specs=in_specs,
        out_specs=out_specs,
        out_shape=out_shape,
        scratch_shapes=[
            pltpu.VMEM((N_ROW_BLK, QK_PAD, ROW_BLK), BF16),
            pltpu.VMEM((LT, QK_PAD), BF16),
        ],
        compiler_params=pltpu.CompilerParams(vmem_limit_bytes=_VMEM_LIMIT),
        name="mla_attention",
    )(*args)
    return (outs[0], outs[1]) if with_ctx else (None, outs[0])


def _block_diag(blocks):
    n, r, c = blocks.shape
    tiled = jnp.tile(blocks.reshape(n * r, c), (1, n))
    on_diag = (jnp.arange(n * r)[:, None] // r) == (jnp.arange(n * c)[None, :] // c)
    return jnp.where(on_diag, tiled, jnp.zeros_like(tiled))


def _s5_matrices(lam_re, lam_im, log_dt, b_re, b_im, c_re, c_im):
    dt = jnp.exp(log_dt)[:, None]
    mag = jnp.exp(lam_re * dt)
    ab_re, ab_im = mag * jnp.cos(lam_im * dt), mag * jnp.sin(lam_im * dt)
    den = lam_re * lam_re + lam_im * lam_im
    nr = ab_re - 1.0
    coef_re = (nr * lam_re + ab_im * lam_im) / den
    coef_im = (ab_im * lam_re - nr * lam_im) / den
    bb_re = coef_re[..., None] * b_re - coef_im[..., None] * b_im
    bb_im = coef_re[..., None] * b_im + coef_im[..., None] * b_re
    to_in = lambda m: _block_diag(jnp.swapaxes(m, 1, 2))
    to_out = lambda m: _block_diag(jnp.swapaxes(m, 1, 2))

    def grouped(re, im, axis):
        parts = []
        for cb in range(S5_NSTATE // CBW):
            parts += [lax.slice_in_dim(re, cb * CBW, (cb + 1) * CBW, axis=axis),
                      lax.slice_in_dim(im, cb * CBW, (cb + 1) * CBW, axis=axis)]
        return jnp.concatenate(parts, axis=axis)

    bb = grouped(to_in(bb_re), to_in(bb_im), 1)
    cc = grouped(to_out(c_re), -to_out(c_im), 0)
    ab = jnp.stack([ab_re.reshape(-1), ab_im.reshape(-1)])
    return bb.astype(BF16), cc.astype(BF16), ab


def _rope_tables():
    n_rows = SEQ // GRID_W
    rows = jnp.repeat(jnp.arange(n_rows, dtype=F32), GRID_W)
    cols = jnp.tile(jnp.arange(GRID_W, dtype=F32), n_rows)
    inv_freq = ROPE_BASE ** (-jnp.arange(0, ROPE_AXIS, 2, dtype=F32) / ROPE_AXIS)
    ang_r = rows[:, None] * inv_freq
    ang_c = cols[:, None] * inv_freq
    cr, sr, cc, sc = jnp.cos(ang_r), jnp.sin(ang_r), jnp.cos(ang_c), jnp.sin(ang_c)
    cos32 = jnp.concatenate([cr, cr, cc, cc], axis=-1)
    sin32 = jnp.concatenate([-sr, sr, -sc, sc], axis=-1)
    cos = jnp.ones((LT, HEAD_PAD), F32).at[CTX_LEN:, MLA_NOPE:MLA_QK].set(cos32)
    sin = jnp.zeros((LT, HEAD_PAD), F32).at[CTX_LEN:, MLA_NOPE:MLA_QK].set(sin32)
    return cos, sin


def _pad_heads(w, used):
    w = jnp.pad(w, ((0, 0), (0, 0), (0, 0), (0, HEAD_PAD - used)))
    return w.reshape(w.shape[0], w.shape[1], MLA_HEADS * HEAD_PAD)


def kernel(x, c, ctx, c_ctx, ada_w, ada_b, norm_g, w_in, s5_lam_re, s5_lam_im, s5_log_dt, s5_b_re,
           s5_b_im, s5_c_re, s5_c_im, s5_d, s5_glu_w, s5_glu_b, lru_conv_w, lru_conv_b, lru_wa,
           lru_ba, lru_wx, lru_bx, lru_lam, mla_q_norm, mla_w_uq, mla_kv_norm, mla_w_ukv, w_out,
           final_g):
    c_pad = jnp.zeros((MOD_ROWS, D_MODEL), F32).at[:BATCH].set(c).at[BATCH].set(c_ctx)
    mod = _mod_tables(c_pad, ada_w, ada_b)

    cosk, sink = _rope_tables()
    cosq_lat, sinq_lat = cosk[CTX_LEN:] * Q_MUL, sink[CTX_LEN:] * Q_MUL

    kr_pad = jnp.zeros((DEPTH, D_MODEL, HEAD_PAD), F32).at[:, :, MLA_NOPE:MLA_QK].set(w_in[:, :, KR_LO:KR_HI])
    w_in_pad = jnp.concatenate([w_in[:, :, 0:KR_LO], kr_pad, w_in[:, :, KR_HI:D_IN]], axis=2).astype(BF16)
    bb, cc, ab = jax.vmap(jax.vmap(_s5_matrices))(s5_lam_re, s5_lam_im, s5_log_dt, s5_b_re, s5_b_im,
                                                  s5_c_re, s5_c_im)
    bd = jax.vmap(jax.vmap(_block_diag))
    wg = (0.5 * jnp.concatenate([bd(lru_wa), bd(lru_wx)], axis=-1)).astype(BF16)
    bg = 0.5 * jnp.concatenate([lru_ba, lru_bx], axis=-1)[:, :, None, :]
    sp = (0.5 * LRU_C) * jax.nn.softplus(-lru_lam)[:, :, None, :]
    w_uq_pad = _pad_heads(mla_w_uq.reshape(DEPTH, MLA_Q_RANK, MLA_HEADS, MLA_QK), MLA_QK).astype(BF16)
    ukv = mla_w_ukv.reshape(DEPTH, MLA_KV_RANK, MLA_HEADS, MLA_NOPE + MLA_V)
    w_ukv_pad = jnp.concatenate([_pad_heads(ukv[..., :MLA_NOPE], MLA_NOPE),
                                 _pad_heads(ukv[..., MLA_NOPE:], MLA_V)], axis=-1).astype(BF16)
    glu_w = s5_glu_w.astype(BF16)
    w_out_b = w_out.astype(BF16)
    row = lambda a: a[:, None, :]

    proj_w = (row(norm_g), w_in_pad)
    x_src = (ctx, x)
    (proj,) = _row_stage(0, x_src, mod, None, proj_w, None, merge=False, project=True)
    for l in range(DEPTH):
        final = l == DEPTH - 1
        y_fwd, y_bwd = (_scan(l, d, proj, bb, cc, ab, lru_conv_w, row(lru_conv_b), wg, bg, sp)
                        for d in range(2))
        ymla_ctx, ymla_lat = _attention(l, proj, cosq_lat, sinq_lat, cosk, sink, row(mla_q_norm), w_uq_pad,
                                        row(mla_kv_norm), w_ukv_pad, with_ctx=not final)
        merge_in = (y_fwd, y_bwd, proj, ymla_ctx, ymla_lat, row(s5_d), glu_w, row(s5_glu_b), w_out_b)
        if final:
            (out,) = _row_stage(l, x_src, mod, merge_in, None, final_g[None], merge=True,
                                project=False, final=True)
            return out
        x_all, proj = _row_stage(l, x_src, mod, merge_in, proj_w, None, merge=True, project=True)
        x_src = (x_all,)
```
